```python
import jax, jax.numpy as jnp
from jax import lax
import numpy as np

D_MODEL = 1024
BATCH = 1
SEQ = 16384
DEPTH = 1
DEC_BATCH = 128
DEC_SEQ = 8
PAST_LEN = 8192
PAGE_SIZE = 128

FOX_HEADS = 8
FOX_HEAD_DIM = 64
FOX_WIDTH = FOX_HEADS * FOX_HEAD_DIM
Q_BLOCK = 128
FORGET_BIAS_INIT = 3.0
RET_HEADS = 4
RET_KEY_DIM = 128
RET_VAL_DIM = 128
RET_QK_WIDTH = RET_HEADS * RET_KEY_DIM
RET_V_WIDTH = RET_HEADS * RET_VAL_DIM
RET_CHUNK = 128
ROPE_BASE = 10000.0
MEM_LEN = 256
MEM_HEADS = 4
MEM_HEAD_DIM = 128
MEM_WIDTH = MEM_HEADS * MEM_HEAD_DIM
N_BRANCH = 3
N_EXPERTS = 32
TOP_K = 4
D_FF = D_MODEL
SWIGLU_LIMIT = 7.0
SWIGLU_ALPHA = 1.702
EXPERT_BLOCK = 128
EPS = 1e-6
IN_SPLIT_WIDTHS = (FOX_WIDTH, FOX_WIDTH, FOX_WIDTH, FOX_HEADS,
                   RET_QK_WIDTH, RET_QK_WIDTH, RET_V_WIDTH, RET_V_WIDTH,
                   MEM_WIDTH, N_BRANCH * D_MODEL)
IN_WIDTH = sum(IN_SPLIT_WIDTHS)
IN_SPLIT_POINTS = tuple(int(v) for v in np.cumsum(IN_SPLIT_WIDTHS)[:-1])

kernel_name = 'fox_retention_memory_moe_hybrid_step'


def rmsnorm(x, g):
    xf = x.astype(jnp.float32)
    xf = xf * lax.rsqrt(jnp.mean(xf * xf, axis=-1, keepdims=True) + EPS)
    return (xf * g.astype(jnp.float32)).astype(x.dtype)


def rotate(x, pos):
    half = x.shape[-1] // 2
    inv_freq = 1.0 / (ROPE_BASE ** jnp.linspace(0.0, 1.0, half, dtype=jnp.float32))
    ang = pos.astype(jnp.float32)[:, None] * inv_freq[None, :]
    cos = jnp.cos(ang)[None, :, None, :]
    sin = jnp.sin(ang)[None, :, None, :]
    xf = x.astype(jnp.float32)
    x1, x2 = xf[..., :half], xf[..., half:]
    return jnp.concatenate([x1 * cos - x2 * sin, x1 * sin + x2 * cos], axis=-1).astype(x.dtype)


def ret_log_decay():
    return jnp.log(1.0 - jnp.exp2(-5.0 - jnp.arange(RET_HEADS, dtype=jnp.float32)))


def mixer_inputs(x, pos, g_mix, w_in, b_forget):
    B, L, _ = x.shape
    xn = rmsnorm(x, g_mix)
    fq, fk, fv, ff, rq, rk, rv, rg, mq, gates = jnp.split(xn @ w_in, IN_SPLIT_POINTS, axis=-1)
    heads = lambda t, h: t.reshape(B, L, h, -1)
    logf = jax.nn.log_sigmoid(ff.astype(jnp.float32) + b_forget.astype(jnp.float32))
    rq = rotate(heads(rq, RET_HEADS), pos)
    rk = rotate(heads(rk, RET_HEADS), pos) * (RET_KEY_DIM ** -0.5)
    return (heads(fq, FOX_HEADS), heads(fk, FOX_HEADS), heads(fv, FOX_HEADS), logf,
            rq, rk, heads(rv, RET_HEADS), rg, heads(mq, MEM_HEADS), gates)


def fox_attend(q, k, v, cq, ck, q_pos, k_pos):
    scale = q.shape[-1] ** -0.5
    s = jnp.einsum('bqhd,bkhd->bhqk', q, k).astype(jnp.float32) * scale
    s = s + jnp.swapaxes(cq, 1, 2)[..., None] - jnp.swapaxes(ck, 1, 2)[:, :, None, :]
    s = jnp.where(k_pos[None, :] <= q_pos[:, None], s, -jnp.inf)
    p = jax.nn.softmax(s, axis=-1).astype(v.dtype)
    return jnp.einsum('bhqk,bkhd->bqhd', p, v)


def fox_prompt(q, k, v, logf):
    B, S, H, d = q.shape
    c = jnp.cumsum(logf, axis=1)
    k_pos = jnp.arange(S)

    def block(i):
        start = i * Q_BLOCK
        qb = lax.dynamic_slice_in_dim(q, start, Q_BLOCK, axis=1)
        cb = lax.dynamic_slice_in_dim(c, start, Q_BLOCK, axis=1)
        return fox_attend(qb, k, v, cb, c, start + jnp.arange(Q_BLOCK), k_pos)

    o = lax.map(block, jnp.arange(S // Q_BLOCK))
    return jnp.swapaxes(o, 0, 1).reshape(B, S, H, d)


def fox_sample(q, k_new, v_new, logf_new, k_past, v_past, logf_past):
    B, T, H, d = q.shape
    past_len = k_past.shape[1]
    scale = d ** -0.5
    cq = jnp.cumsum(logf_new, axis=1)
    lp = logf_past.astype(jnp.float32)
    suffix = lax.cumsum(lp, axis=1, reverse=True) - lp
    cq_t = jnp.swapaxes(cq, 1, 2)
    s_past = (jnp.einsum('bqhd,bkhd->bhqk', q, k_past.astype(q.dtype)).astype(jnp.float32) * scale
              + cq_t[..., None] + jnp.swapaxes(suffix, 1, 2)[:, :, None, :])
    s_new = (jnp.einsum('bqhd,bkhd->bhqk', q, k_new).astype(jnp.float32) * scale
             + cq_t[..., None] - cq_t[:, :, None, :])
    causal = jnp.arange(T)[None, :] <= jnp.arange(T)[:, None]
    s_new = jnp.where(causal, s_new, -jnp.inf)
    p = jax.nn.softmax(jnp.concatenate([s_past, s_new], axis=-1), axis=-1).astype(v_new.dtype)
    return (jnp.einsum('bhqk,bkhd->bqhd', p[..., :past_len], v_past.astype(v_new.dtype))
            + jnp.einsum('bhqk,bkhd->bqhd', p[..., past_len:], v_new))


def retention(q, k, v, state0):
    B, L, H, _ = q.shape
    C = min(RET_CHUNK, L)
    N = L // C
    f32 = jnp.float32
    lg = ret_log_decay()
    idx = jnp.arange(C, dtype=f32)
    rel = idx[:, None] - idx[None, :]
    intra_decay = jnp.where(rel >= 0, jnp.exp(jnp.maximum(rel, 0.0)[None] * lg[:, None, None]), 0.0)
    q_decay = jnp.exp((idx + 1.0)[:, None] * lg[None, :])
    k_decay = jnp.exp((C - 1.0 - idx)[:, None] * lg[None, :])
    chunk_decay = jnp.exp(C * lg)

    def step(state, chunk):
        qc, kc, vc = (t.astype(f32) for t in chunk)
        a = jnp.einsum('bihd,bjhd->bhij', qc, kc) * intra_decay
        o = (jnp.einsum('bhij,bjhe->bihe', a, vc)
             + jnp.einsum('bihd,bhde->bihe', qc, state) * q_decay[None, :, :, None])
        state = (chunk_decay[None, :, None, None] * state
                 + jnp.einsum('bjhd,bjhe->bhde', kc * k_decay[None, :, :, None], vc))
        return state, o

    chunks = tuple(t.reshape(B, N, C, H, t.shape[-1]).swapaxes(0, 1) for t in (q, k, v))
    state, o = lax.scan(step, state0.astype(f32), chunks)
    return o.swapaxes(0, 1).reshape(B, L, H, -1), state


def memory_kv(mem, g_mem, w_mem_kv):
    B, M, _ = mem.shape
    mk, mv = jnp.split(rmsnorm(mem, g_mem) @ w_mem_kv, 2, axis=-1)
    return mk.reshape(B, M, MEM_HEADS, MEM_HEAD_DIM), mv.reshape(B, M, MEM_HEADS, MEM_HEAD_DIM)


def mem_attend(q, mk, mv):
    scale = q.shape[-1] ** -0.5
    s = jnp.einsum('blhd,bmhd->bhlm', q, mk.astype(q.dtype)).astype(jnp.float32) * scale
    p = jax.nn.softmax(s, axis=-1).astype(q.dtype)
    return jnp.einsum('bhlm,bmhd->blhd', p, mv.astype(q.dtype))


def mixer_output(x, fox_o, ret_o, ret_g, mem_o, gates, w_up_fox, w_up_ret, w_up_mem, w_out):
    B, L, _ = x.shape
    rf = ret_o.astype(jnp.float32)
    rn = rf * lax.rsqrt(jnp.mean(rf * rf, axis=-1, keepdims=True) + EPS)
    ret_y = (rn.reshape(B, L, RET_V_WIDTH) * jax.nn.silu(ret_g.astype(jnp.float32))).astype(x.dtype)
    g = jax.nn.sigmoid(gates.astype(jnp.float32)).reshape(B, L, N_BRANCH, D_MODEL)
    merged = (g[:, :, 0] * (fox_o.reshape(B, L, FOX_WIDTH) @ w_up_fox)
              + g[:, :, 1] * (ret_y @ w_up_ret)
              + g[:, :, 2] * (mem_o.reshape(B, L, MEM_WIDTH) @ w_up_mem))
    return x + merged.astype(x.dtype) @ w_out


def moe(xt, w_router, b_router, w_ff1, b_ff1, w_ff2, b_ff2):
    T, D = xt.shape
    f32 = jnp.float32
    logits = (xt @ w_router).astype(f32) + b_router.astype(f32)
    top_val, top_idx = lax.top_k(logits, TOP_K)
    gate = jax.nn.softmax(top_val, axis=-1)
    A = T * TOP_K
    expert = top_idx.reshape(A)
    token = jnp.arange(A, dtype=jnp.int32) // TOP_K
    weight = gate.reshape(A)
    order = jnp.argsort(expert)
    e_s, t_s, w_s = expert[order], token[order], weight[order]
    counts = jnp.bincount(expert, length=N_EXPERTS)
    padded = (counts + EXPERT_BLOCK - 1) // EXPERT_BLOCK * EXPERT_BLOCK
    pad_end = jnp.cumsum(padded)
    dest = (pad_end - padded)[e_s] + jnp.arange(A) - (jnp.cumsum(counts) - counts)[e_s]
    n_blocks = -(-(A + N_EXPERTS * (EXPERT_BLOCK - 1)) // EXPERT_BLOCK)
    rows = n_blocks * EXPERT_BLOCK
    row_token = jnp.full((rows,), T, jnp.int32).at[dest].set(t_s)
    row_weight = jnp.zeros((rows,), f32).at[dest].set(w_s)
    block_expert = jnp.minimum(
        jnp.searchsorted(pad_end, jnp.arange(n_blocks) * EXPERT_BLOCK, side='right'), N_EXPERTS - 1)
    x_rows = jnp.concatenate([xt, jnp.zeros((1, D), xt.dtype)], axis=0)[row_token]
    x_rows = x_rows.reshape(n_blocks, EXPERT_BLOCK, D)

    def expert_block(args):
        xb, e = args
        hdn = (xb @ w_ff1[e] + b_ff1[e]).astype(f32)
        glu, lin = jnp.split(hdn, 2, axis=-1)
        glu = jnp.minimum(glu, SWIGLU_LIMIT)
        lin = jnp.clip(lin, -SWIGLU_LIMIT, SWIGLU_LIMIT)
        act = (glu * jax.nn.sigmoid(SWIGLU_ALPHA * glu) * (lin + 1.0)).astype(xb.dtype)
        return act @ w_ff2[e] + b_ff2[e]

    y_rows = lax.map(expert_block, (x_rows, block_expert)).reshape(rows, D)
    y = jax.ops.segment_sum(y_rows.astype(f32) * row_weight[:, None], row_token, num_segments=T + 1)
    return y[:T]


def decoder_layer(x, pos, fox_fn, ret_state0, mem_k, mem_v, lw):
    (g_mix, w_in, b_forget, w_up_fox, w_up_ret, w_up_mem, w_out,
     g_moe, w_router, b_router, w_ff1, b_ff1, w_ff2, b_ff2) = lw
    fq, fk, fv, logf, rq, rk, rv, rg, mq, gates = mixer_inputs(x, pos, g_mix, w_in, b_forget)
    fox_o = fox_fn(fq, fk, fv, logf)
    ret_o, ret_state = retention(rq, rk, rv, ret_state0)
    mem_o = mem_attend(mq, mem_k, mem_v)
    h = mixer_output(x, fox_o, ret_o, rg, mem_o, gates, w_up_fox, w_up_ret, w_up_mem, w_out)
    B, L, D = h.shape
    y = moe(rmsnorm(h, g_moe).reshape(B * L, D), w_router, b_router, w_ff1, b_ff1, w_ff2, b_ff2)
    out = h + y.reshape(B, L, D).astype(h.dtype)
    return out, fk, fv, logf.astype(x.dtype), ret_state.astype(x.dtype)


def setup_inputs(seed: int = 0) -> dict:
    key = jax.random.key(seed)
    ks = iter(jax.random.split(key, 32))
    f32 = jnp.float32

    def normal(shape, scale=1.0):
        return jax.random.normal(next(ks), shape, f32) * scale

    n_pages = PAST_LEN // PAGE_SIZE
    n_used = DEC_BATCH * n_pages
    n_pool = n_used + max(1, n_used // 4)
    page_table = jax.random.permutation(next(ks), n_pool)[:n_used].reshape(DEC_BATCH, n_pages).astype(jnp.int32)
    return {
        'x_prompt': normal((BATCH, SEQ, D_MODEL)),
        'x_sample': normal((DEC_BATCH, DEC_SEQ, D_MODEL)),
        'mem_prompt': normal((BATCH, MEM_LEN, D_MODEL)),
        'cache_fox_k': normal((DEPTH, n_pool, PAGE_SIZE, FOX_HEADS, FOX_HEAD_DIM)),
        'cache_fox_v': normal((DEPTH, n_pool, PAGE_SIZE, FOX_HEADS, FOX_HEAD_DIM)),
        'cache_fox_logf': jax.nn.log_sigmoid(FORGET_BIAS_INIT + normal((DEPTH, n_pool, PAGE_SIZE, FOX_HEADS))),
        'state_ret': normal((DEPTH, DEC_BATCH, RET_HEADS, RET_KEY_DIM, RET_VAL_DIM), 0.5),
        'cache_mem_k': normal((DEPTH, DEC_BATCH, MEM_LEN, MEM_HEADS, MEM_HEAD_DIM)),
        'cache_mem_v': normal((DEPTH, DEC_BATCH, MEM_LEN, MEM_HEADS, MEM_HEAD_DIM)),
        'page_table': page_table,
        'g_mix': 1.0 + normal((DEPTH, D_MODEL), 0.02),
        'w_in': normal((DEPTH, D_MODEL, IN_WIDTH), D_MODEL ** -0.5),
        'b_forget': FORGET_BIAS_INIT + normal((DEPTH, FOX_HEADS), 0.5),
        'w_up_fox': normal((DEPTH, FOX_WIDTH, D_MODEL), FOX_WIDTH ** -0.5),
        'w_up_ret': normal((DEPTH, RET_V_WIDTH, D_MODEL), RET_V_WIDTH ** -0.5),
        'w_up_mem': normal((DEPTH, MEM_WIDTH, D_MODEL), MEM_WIDTH ** -0.5),
        'w_out': normal((DEPTH, D_MODEL, D_MODEL), D_MODEL ** -0.5),
        'g_mem': 1.0 + normal((DEPTH, D_MODEL), 0.02),
        'w_mem_kv': normal((DEPTH, D_MODEL, 2 * MEM_WIDTH), D_MODEL ** -0.5),
        'g_moe': 1.0 + normal((DEPTH, D_MODEL), 0.02),
        'w_router': normal((DEPTH, D_MODEL, N_EXPERTS), D_MODEL ** -0.5),
        'b_router': normal((DEPTH, N_EXPERTS), 0.01),
        'w_ff1': normal((DEPTH, N_EXPERTS, D_MODEL, 2 * D_FF), D_MODEL ** -0.5),
        'b_ff1': normal((DEPTH, N_EXPERTS, 2 * D_FF), 0.02),
        'w_ff2': normal((DEPTH, N_EXPERTS, D_FF, D_MODEL), D_FF ** -0.5),
        'b_ff2': normal((DEPTH, N_EXPERTS, D_MODEL), 0.02),
        'g_final': 1.0 + normal((D_MODEL,), 0.02),
    }


def reference(x_prompt, x_sample, mem_prompt, cache_fox_k, cache_fox_v, cache_fox_logf, state_ret,
              cache_mem_k, cache_mem_v, page_table, g_mix, w_in, b_forget, w_up_fox, w_up_ret, w_up_mem,
              w_out, g_mem, w_mem_kv, g_moe, w_router, b_router, w_ff1, b_ff1, w_ff2, b_ff2, g_final):
    B, S, _ = x_prompt.shape
    DB, T, _ = x_sample.shape
    past_len = page_table.shape[1] * cache_fox_k.shape[2]
    pos_prompt = jnp.arange(S)
    pos_sample = past_len + jnp.arange(T)
    xp, xs = x_prompt, x_sample
    fk_p, fv_p, lf_p, rs_p, mk_p, mv_p = [], [], [], [], [], []
    fk_s, fv_s, lf_s, rs_s = [], [], [], []
    for l in range(DEPTH):
        lw = (g_mix[l], w_in[l], b_forget[l], w_up_fox[l], w_up_ret[l], w_up_mem[l], w_out[l],
              g_moe[l], w_router[l], b_router[l], w_ff1[l], b_ff1[l], w_ff2[l], b_ff2[l])
        mk, mv = memory_kv(mem_prompt, g_mem[l], w_mem_kv[l])
        ret0 = jnp.zeros((B, RET_HEADS, RET_KEY_DIM, RET_VAL_DIM), jnp.float32)
        xp, fk, fv, lf, rs = decoder_layer(xp, pos_prompt, fox_prompt, ret0, mk, mv, lw)
        fk_p.append(fk); fv_p.append(fv); lf_p.append(lf); rs_p.append(rs); mk_p.append(mk); mv_p.append(mv)
        k_past = cache_fox_k[l][page_table].reshape(DB, past_len, FOX_HEADS, FOX_HEAD_DIM)
        v_past = cache_fox_v[l][page_table].reshape(DB, past_len, FOX_HEADS, FOX_HEAD_DIM)
        lf_past = cache_fox_logf[l][page_table].reshape(DB, past_len, FOX_HEADS)
        fox_fn = lambda q, k, v, lf, kp=k_past, vp=v_past, lp=lf_past: fox_sample(q, k, v, lf, kp, vp, lp)
        xs, fk, fv, lf, rs = decoder_layer(xs, pos_sample, fox_fn, state_ret[l], cache_mem_k[l], cache_mem_v[l], lw)
        fk_s.append(fk); fv_s.append(fv); lf_s.append(lf); rs_s.append(rs)
    y_prompt = rmsnorm(xp, g_final)
    y_sample = rmsnorm(xs, g_final)
    return (y_prompt, y_sample,
            jnp.stack(fk_p), jnp.stack(fv_p), jnp.stack(lf_p), jnp.stack(rs_p), jnp.stack(mk_p), jnp.stack(mv_p),
            jnp.stack(fk_s), jnp.stack(fv_s), jnp.stack(lf_s), jnp.stack(rs_s))
```

```python
import functools
import math

import jax
import jax.numpy as jnp
import numpy as np
from jax import lax
from jax.experimental import pallas as pl
from jax.experimental.pallas import tpu as pltpu

F32 = jnp.float32
BF16 = jnp.bfloat16
I32 = jnp.int32

FOX_HEADS = 8
FOX_DIM = 64
RET_HEADS = 4
RET_DIM = 128
RET_CHUNK = 128
MEM_HEADS = 4
MEM_DIM = 128
N_BRANCH = 3
TOP_K = 4
ROPE_BASE = 10000.0
SWIGLU_LIMIT = 7.0
SWIGLU_ALPHA = 1.702
EPS = 1e-6

LOG2E = 1.4426950408889634
NEG = -1e30
VMEM_LIMIT = 48 * 1024 * 1024

HI = lax.Precision.HIGHEST
NT = (((1,), (1,)), ((), ()))
TN = (((0,), (0,)), ((), ()))


def _params(sem, vmem=VMEM_LIMIT):
    return pltpu.CompilerParams(dimension_semantics=sem, vmem_limit_bytes=vmem)


def _rms(x, g):
    return x * lax.rsqrt(jnp.mean(x * x, axis=-1, keepdims=True) + EPS) * g


def _log_sigmoid(x):
    return jnp.minimum(x, 0.0) - jnp.log(1.0 + jnp.exp(-jnp.abs(x)))


def _sigmoid(x):
    return 1.0 / (1.0 + jnp.exp(-x))


def _split3(x):
    hi = x.astype(BF16)
    r = x - hi.astype(F32)
    mid = r.astype(BF16)
    lo = (r - mid.astype(F32)).astype(BF16)
    return hi, mid, lo


def _full(shape):
    nd = len(shape)
    return pl.BlockSpec(shape, lambda *_: (0,) * nd)


def _proj_fox_kernel(x_ref, g_ref, w_ref, wf_ref, bf_ref, fq_ref, fk_ref, fv_ref, lf_ref):
    xn = _rms(x_ref[...], g_ref[...]).astype(BF16)
    y = jnp.dot(xn, w_ref[...], preferred_element_type=F32)
    w = fq_ref.shape[1]
    fq_ref[...] = y[:, :w]
    fk_ref[...] = y[:, w:2 * w]
    fv_ref[...] = y[:, 2 * w:]
    ff = jnp.dot(xn, wf_ref[...], preferred_element_type=F32)
    lf_ref[...] = _log_sigmoid(ff + bf_ref[...])[:, :lf_ref.shape[1]]


def _rotate(y, cos, sin):
    outs = []
    for h in range(RET_HEADS):
        yh = y[:, h * RET_DIM:(h + 1) * RET_DIM]
        outs.append(yh * cos + pltpu.roll(yh, RET_DIM // 2, axis=1) * sin)
    return jnp.concatenate(outs, axis=1)


def _proj_ret_kernel(x_ref, g_ref, w_ref, wm_ref, cos_ref, sin_ref, rq_ref, rk_ref, rv_ref, rg_ref, mq_ref):
    xn = _rms(x_ref[...], g_ref[...]).astype(BF16)
    y = jnp.dot(xn, w_ref[...], preferred_element_type=F32)
    w = rq_ref.shape[1]
    cos, sin = cos_ref[...], sin_ref[...]
    rq_ref[...] = _rotate(y[:, :w], cos, sin)
    rk_ref[...] = _rotate(y[:, w:2 * w], cos, sin) * (RET_DIM ** -0.5)
    rv_ref[...] = y[:, 2 * w:3 * w]
    rg_ref[...] = y[:, 3 * w:]
    mq_ref[...] = jnp.dot(xn, wm_ref[...], preferred_element_type=F32)


def _proj_gate_kernel(x_ref, g_ref, w_ref, o_ref):
    xn = _rms(x_ref[...], g_ref[...]).astype(BF16)
    o_ref[...] = jnp.dot(xn, w_ref[...], preferred_element_type=F32)


def _input_projection(x, g, wparts, b_forget_pad, cos, sin, tm):
    R, D = x.shape
    w_fox, w_ff, w_ret, w_mem, w_gate = wparts
    fw = FOX_HEADS * FOX_DIM
    rw = RET_HEADS * RET_DIM
    mw = MEM_HEADS * MEM_DIM
    row = lambda n: pl.BlockSpec((tm, n), lambda i: (i, 0))
    sds = lambda n: jax.ShapeDtypeStruct((R, n), F32)
    grid = (R // tm,)
    fq, fk, fv, lf = pl.pallas_call(
        _proj_fox_kernel, grid=grid,
        in_specs=[row(D), _full((1, D)), _full(w_fox.shape), _full(w_ff.shape), _full((1, 128))],
        out_specs=[row(fw), row(fw), row(fw), row(FOX_HEADS)],
        out_shape=[sds(fw), sds(fw), sds(fw), sds(FOX_HEADS)],
        compiler_params=_params(("parallel",)), name="proj_fox",
    )(x, g, w_fox, w_ff, b_forget_pad)
    rq, rk, rv, rg, mq = pl.pallas_call(
        _proj_ret_kernel, grid=grid,
        in_specs=[row(D), _full((1, D)), _full(w_ret.shape), _full(w_mem.shape), row(RET_DIM), row(RET_DIM)],
        out_specs=[row(rw), row(rw), row(rw), row(rw), row(mw)],
        out_shape=[sds(rw), sds(rw), sds(rw), sds(rw), sds(mw)],
        compiler_params=_params(("parallel",)), name="proj_ret",
    )(x, g, w_ret, w_mem, cos, sin)
    tg = min(tm, 256)
    gates = pl.pallas_call(
        _proj_gate_kernel, grid=(R // tg,),
        in_specs=[pl.BlockSpec((tg, D), lambda i: (i, 0)), _full((1, D)), _full(w_gate.shape)],
        out_specs=pl.BlockSpec((tg, w_gate.shape[1]), lambda i: (i, 0)),
        out_shape=sds(w_gate.shape[1]),
        compiler_params=_params(("parallel",)), name="proj_gate",
    )(x, g, w_gate)
    return fq, fk, fv, lf, rq, rk, rv, rg, mq, gates


def _memkv_kernel(x_ref, g_ref, w_ref, mk_ref, mv_ref):
    xn = _rms(x_ref[...], g_ref[...]).astype(BF16)
    y = jnp.dot(xn, w_ref[...], preferred_element_type=F32)
    w = mk_ref.shape[1]
    mk_ref[...] = y[:, :w]
    mv_ref[...] = y[:, w:]


def _memory_kv(mem, g, w):
    M, D = mem.shape
    mw = MEM_HEADS * MEM_DIM
    return pl.pallas_call(
        _memkv_kernel, grid=(1,),
        in_specs=[_full((M, D)), _full((1, D)), _full(w.shape)],
        out_specs=[_full((M, mw)), _full((M, mw))],
        out_shape=[jax.ShapeDtypeStruct((M, mw), F32)] * 2,
        compiler_params=_params(("arbitrary",)), name="memory_kv",
    )(mem, g, w)


def _fox_prep_kernel(fq_ref, fk_ref, fv_ref, lf_ref, tri_ref, qa_ref, ka_ref, v_ref, carry_ref):
    @pl.when(pl.program_id(0) == 0)
    def _():
        carry_ref[...] = jnp.zeros_like(carry_ref)

    tm = fq_ref.shape[0]
    c = jnp.dot(tri_ref[...], lf_ref[...], preferred_element_type=F32, precision=HI) + carry_ref[...]
    carry_ref[...] = c[tm - 1:tm, :]
    hi, mid, lo = (p.astype(F32) for p in _split3(c * LOG2E))
    lane = lax.broadcasted_iota(I32, (tm, 2 * FOX_DIM), 1)
    d = FOX_DIM
    fq, fk, fv = fq_ref[...], fk_ref[...], fv_ref[...]
    for h in range(FOX_HEADS):
        g = h // 2
        sl = slice(g * 2 * d, (g + 1) * 2 * d)
        qg, kg, vg = fq[:, sl], fk[:, sl], fv[:, sl]
        if h % 2:
            qg, kg, vg = (pltpu.roll(t, d, axis=1) for t in (qg, kg, vg))
        ch, cm, cl = hi[:, h:h + 1], mid[:, h:h + 1], lo[:, h:h + 1]
        one = jnp.where((lane >= d + 3) & (lane < d + 6), 1.0, 0.0)
        qa = jnp.where(lane < d, qg * (d ** -0.5 * LOG2E),
                       jnp.where(lane == d, ch, jnp.where(lane == d + 1, cm, jnp.where(lane == d + 2, cl, one))))
        onek = jnp.where((lane >= d) & (lane < d + 3), 1.0, 0.0)
        ka = jnp.where(lane < d, kg,
                       jnp.where(lane == d + 3, -ch, jnp.where(lane == d + 4, -cm, jnp.where(lane == d + 5, -cl, onek))))
        qa_ref[h] = qa.astype(BF16)
        ka_ref[h] = ka.astype(BF16)
        v_ref[h] = vg[:, :d].astype(BF16)


def _fox_prep(fq, fk, fv, lf, tm):
    S, W = fq.shape
    tri = jnp.tril(jnp.ones((tm, tm), F32))
    row = lambda n: pl.BlockSpec((tm, n), lambda i: (i, 0))
    hm = lambda n: pl.BlockSpec((FOX_HEADS, tm, n), lambda i: (0, i, 0))
    return pl.pallas_call(
        _fox_prep_kernel, grid=(S // tm,),
        in_specs=[row(W), row(W), row(W), row(FOX_HEADS), _full((tm, tm))],
        out_specs=[hm(2 * FOX_DIM), hm(2 * FOX_DIM), hm(FOX_DIM)],
        out_shape=[jax.ShapeDtypeStruct((FOX_HEADS, S, 2 * FOX_DIM), BF16)] * 2
        + [jax.ShapeDtypeStruct((FOX_HEADS, S, FOX_DIM), BF16)],
        scratch_shapes=[pltpu.VMEM((1, FOX_HEADS), F32)],
        compiler_params=_params(("arbitrary",)), name="fox_prep",
    )(fq, fk, fv, lf, tri)


def _fox_prompt_kernel(q_ref, k_ref, v_ref, o_ref, *, tq, tk):
    i = pl.program_id(1)
    q = q_ref[...]
    nfull = (i * tq) // tk

    def step(j, carry, masked):
        m, l, acc = carry
        k0 = pl.multiple_of(j * tk, tk)
        k = k_ref[pl.ds(k0, tk), :]
        v = v_ref[pl.ds(k0, tk), :]
        s = lax.dot_general(q, k, NT, preferred_element_type=F32)
        if masked:
            rowg = i * tq + lax.broadcasted_iota(I32, (tq, tk), 0)
            colg = k0 + lax.broadcasted_iota(I32, (tq, tk), 1)
            s = jnp.where(colg <= rowg, s, NEG)
        m_new = jnp.maximum(m, jnp.max(s, axis=1, keepdims=True))
        alpha = jnp.exp2(m - m_new)
        p = jnp.exp2(s - m_new)
        l = alpha * l + jnp.sum(p, axis=1, keepdims=True)
        acc = alpha * acc + jnp.dot(p.astype(BF16), v, preferred_element_type=F32)
        return m_new, l, acc

    init = (jnp.full((tq, 1), NEG, F32), jnp.zeros((tq, 1), F32), jnp.zeros((tq, FOX_DIM), F32))
    carry = lax.fori_loop(0, nfull, lambda j, c: step(j, c, False), init)
    m, l, acc = step(nfull, carry, True)
    o_ref[...] = (acc / l).astype(o_ref.dtype)


def _fox_prompt(qa, ka, v, tq, tk):
    H, S, A = qa.shape
    return pl.pallas_call(
        functools.partial(_fox_prompt_kernel, tq=tq, tk=tk), grid=(H, S // tq),
        in_specs=[pl.BlockSpec((None, tq, A), lambda h, i: (h, i, 0)),
                  pl.BlockSpec((None, S, A), lambda h, i: (h, 0, 0)),
                  pl.BlockSpec((None, S, FOX_DIM), lambda h, i: (h, 0, 0))],
        out_specs=pl.BlockSpec((None, tq, FOX_DIM), lambda h, i: (h, i, 0)),
        out_shape=jax.ShapeDtypeStruct((H, S, FOX_DIM), BF16),
        compiler_params=_params(("parallel", "arbitrary")), name="fox_prompt",
    )(qa, ka, v)


def _fox_sample_kernel(pt_ref, q_ref, kn_ref, vn_ref, lfn_ref, lfnt_ref, mt_ref, mb_ref, *rest, pp, page):
    k_refs, v_refs, l_refs = rest[:pp], rest[pp:2 * pp], rest[2 * pp:3 * pp]
    o_ref = rest[3 * pp]
    m_ref, l_ref, acc_ref, carry_ref, qs_ref, mrow_ref = rest[3 * pp + 1:]
    j = pl.program_id(1)
    H, d, T = FOX_HEADS, FOX_DIM, q_ref.shape[0]
    HT = H * T

    @pl.when(j == 0)
    def _():
        qsc = q_ref[...] * (d ** -0.5 * LOG2E)
        qs_ref[...] = jnp.concatenate([qsc[:, h * d:(h + 1) * d] for h in range(H)], axis=0).astype(BF16)
        lfn, lfnt = lfn_ref[...], lfnt_ref[...]
        ti = lax.broadcasted_iota(I32, (T, T), 0)
        tj = lax.broadcasted_iota(I32, (T, T), 1)
        low = jnp.where(tj <= ti, 1.0, 0.0)
        up = jnp.where(ti <= tj, 1.0, 0.0)
        cq = jnp.zeros((T, H), F32)
        cqt = jnp.zeros((H, T), F32)
        for t in range(T):
            cq = cq + low[:, t:t + 1] * lfn[t:t + 1, :]
            cqt = cqt + lfnt[:, t:t + 1] * up[t:t + 1, :]
        bias = jnp.concatenate([cq[:, h:h + 1] - cqt[h:h + 1, :] for h in range(H)], axis=0) * LOG2E
        rowc = jnp.concatenate([cq[:, h:h + 1] for h in range(H)], axis=0) * LOG2E
        mrow_ref[...] = mb_ref[...] + rowc
        r_i = lax.broadcasted_iota(I32, (HT, H * d), 0)
        c_i = lax.broadcasted_iota(I32, (HT, H * d), 1)
        qbd = jnp.where(c_i // d == r_i // T, jnp.concatenate([qsc] * H, axis=0), 0.0).astype(BF16)
        s = lax.dot_general(qbd, kn_ref[...].astype(BF16), NT, preferred_element_type=F32) + bias
        rr = lax.broadcasted_iota(I32, (HT, T), 0)
        cc = lax.broadcasted_iota(I32, (HT, T), 1)
        s = jnp.where(cc <= rr % T, s, NEG)
        m0 = jnp.max(s, axis=1, keepdims=True)
        p = jnp.exp2(s - m0)
        m_ref[...] = m0
        l_ref[...] = jnp.sum(p, axis=1, keepdims=True)
        of = jnp.dot(p.astype(BF16), vn_ref[...].astype(BF16), preferred_element_type=F32)
        acc_ref[...] = jnp.concatenate([of[h * T:(h + 1) * T, h * d:(h + 1) * d] for h in range(H)], axis=0)
        carry_ref[...] = jnp.zeros_like(carry_ref)

    qs = qs_ref[...]
    mrow = mrow_ref[...]
    mt = mt_ref[...]
    nl = page * H
    for r in range(pp):
        pieces = _split3(l_refs[r][...])
        rr = sum(lax.dot_general(pc, mt, TN, preferred_element_type=F32) for pc in pieces)
        carry = carry_ref[...]
        rs = (rr[:, :nl] + carry[:, 0:1]) * LOG2E
        carry_ref[...] = carry + rr[:, nl:]
        bias = jnp.concatenate([jnp.broadcast_to(rs[h:h + 1, :], (T, nl)) for h in range(H)], axis=0)
        kb = k_refs[r][...].astype(BF16)
        vb = v_refs[r][...].astype(BF16)
        s = lax.dot_general(qs, kb, NT, preferred_element_type=F32) + mrow + bias
        m = m_ref[...]
        m_new = jnp.maximum(m, jnp.max(s, axis=1, keepdims=True))
        alpha = jnp.exp2(m - m_new)
        p = jnp.exp2(s - m_new)
        l_ref[...] = alpha * l_ref[...] + jnp.sum(p, axis=1, keepdims=True)
        acc_ref[...] = alpha * acc_ref[...] + jnp.dot(p.astype(BF16), vb, preferred_element_type=F32)
        m_ref[...] = m_new

    @pl.when(j == pl.num_programs(1) - 1)
    def _():
        o_ref[...] = (acc_ref[...] / l_ref[...]).reshape(H, T, d)


def _fox_sample(q, kn, vn, lfn, page_table, cache_k, cache_v, cache_lf, pp):
    R, W = q.shape
    DB, NP = page_table.shape
    T = R // DB
    n_pool, page, H, d = cache_k.shape
    nl = page * H
    ck = cache_k.reshape(n_pool, nl, d)
    cv = cache_v.reshape(n_pool, nl, d)
    lfnt = jnp.swapaxes(lfn.reshape(DB, T, H), 1, 2)
    tcol = np.arange(nl) // H
    mt = np.concatenate([(np.arange(page)[:, None] > tcol[None, :]), np.ones((page, 128), bool)], axis=1)
    mt = jnp.asarray(mt, BF16)
    mb = np.where((np.arange(nl) % H)[None, :] == (np.arange(H * T) // T)[:, None], 0.0, NEG)
    mb = jnp.asarray(mb, F32)
    pt = page_table.reshape(-1).astype(I32)

    def page_spec(shape, r):
        return pl.BlockSpec((None,) + shape, lambda b, j, pt: (pt[b * NP + NP - 1 - (j * pp + r)], 0, 0))

    tok = lambda n: pl.BlockSpec((T, n), lambda b, j, pt: (b, 0))
    cst = lambda shape: pl.BlockSpec(shape, lambda b, j, pt: (0,) * len(shape))
    in_specs = [tok(W), tok(W), tok(W), tok(H), pl.BlockSpec((None, H, T), lambda b, j, pt: (b, 0, 0)),
                cst(mt.shape), cst(mb.shape)]
    in_specs += [page_spec((nl, d), r) for r in range(pp)]
    in_specs += [page_spec((nl, d), r) for r in range(pp)]
    in_specs += [page_spec((page, H), r) for r in range(pp)]
    grid_spec = pltpu.PrefetchScalarGridSpec(
        num_scalar_prefetch=1, grid=(DB, NP // pp), in_specs=in_specs,
        out_specs=pl.BlockSpec((H, T, d), lambda b, j, pt: (0, b, 0)),
        scratch_shapes=[pltpu.VMEM((H * T, 1), F32), pltpu.VMEM((H * T, 1), F32), pltpu.VMEM((H * T, d), F32),
                        pltpu.VMEM((H, 128), F32), pltpu.VMEM((H * T, d), BF16), pltpu.VMEM((H * T, nl), F32)])
    return pl.pallas_call(
        functools.partial(_fox_sample_kernel, pp=pp, page=page), grid_spec=grid_spec,
        out_shape=jax.ShapeDtypeStruct((H, R, d), F32),
        compiler_params=_params(("parallel", "arbitrary")), name="fox_sample",
    )(pt, q, kn, vn, lfn, lfnt, mt, mb, *([ck] * pp), *([cv] * pp), *([cache_lf] * pp))


def _ret_consts(C):
    lg = jnp.log(1.0 - jnp.exp2(-5.0 - jnp.arange(RET_HEADS, dtype=F32)))
    idx = jnp.arange(C, dtype=F32)
    rel = idx[:, None] - idx[None, :]
    intra = jnp.where(rel >= 0, jnp.exp(jnp.maximum(rel, 0.0)[None] * lg[:, None, None]), 0.0)
    qd = jnp.exp((idx + 1.0)[None, :] * lg[:, None])
    kd = jnp.exp((C - 1.0 - idx)[None, :] * lg[:, None])
    cd = jnp.exp(C * lg)
    rep = lambda t: jnp.broadcast_to(t[:, :, None], t.shape + (RET_DIM,))
    return intra, rep(qd), rep(kd), jnp.broadcast_to(cd[:, None, None], (RET_HEADS, 1, RET_DIM))


def _ret_heads(q, k, v, state_of, intra_ref, qd_ref, kd_ref, cd_ref):
    outs, states = [], []
    for h in range(RET_HEADS):
        sl = slice(h * RET_DIM, (h + 1) * RET_DIM)
        qh, kh, vh = q[:, sl].astype(BF16), k[:, sl], v[:, sl].astype(BF16)
        st = state_of(h)
        a = lax.dot_general(qh, kh.astype(BF16), NT, preferred_element_type=F32) * intra_ref[h]
        o = (jnp.dot(a.astype(BF16), vh, preferred_element_type=F32)
             + jnp.dot(qh, st.astype(BF16), preferred_element_type=F32) * qd_ref[h])
        kdec = (kh * kd_ref[h]).astype(BF16)
        states.append(cd_ref[h] * st + lax.dot_general(kdec, vh, TN, preferred_element_type=F32))
        outs.append(o)
    return jnp.concatenate(outs, axis=1), states


def _ret_prompt_kernel(q_ref, k_ref, v_ref, intra_ref, qd_ref, kd_ref, cd_ref, o_ref, so_ref, st_ref):
    @pl.when(pl.program_id(0) == 0)
    def _():
        st_ref[...] = jnp.zeros_like(st_ref)

    o, states = _ret_heads(q_ref[...], k_ref[...], v_ref[...], lambda h: st_ref[h],
                           intra_ref, qd_ref, kd_ref, cd_ref)
    o_ref[...] = o
    for h in range(RET_HEADS):
        st_ref[h] = states[h]

    @pl.when(pl.program_id(0) == pl.num_programs(0) - 1)
    def _():
        so_ref[...] = st_ref[...]


def _ret_prompt(rq, rk, rv):
    S, W = rq.shape
    C = min(RET_CHUNK, S)
    consts = _ret_consts(C)
    row = pl.BlockSpec((C, W), lambda i: (i, 0))
    st_shape = (RET_HEADS, RET_DIM, RET_DIM)
    return pl.pallas_call(
        _ret_prompt_kernel, grid=(S // C,),
        in_specs=[row, row, row] + [_full(c.shape) for c in consts],
        out_specs=[row, _full(st_shape)],
        out_shape=[jax.ShapeDtypeStruct((S, W), F32), jax.ShapeDtypeStruct(st_shape, F32)],
        scratch_shapes=[pltpu.VMEM(st_shape, F32)],
        compiler_params=_params(("arbitrary",)), name="ret_prompt",
    )(rq, rk, rv, *consts)


def _ret_sample_kernel(q_ref, k_ref, v_ref, s0_ref, intra_ref, qd_ref, kd_ref, cd_ref, o_ref, so_ref):
    o, states = _ret_heads(q_ref[...], k_ref[...], v_ref[...], lambda h: s0_ref[h],
                           intra_ref, qd_ref, kd_ref, cd_ref)
    o_ref[...] = o
    for h in range(RET_HEADS):
        so_ref[h] = states[h]


def _ret_sample(rq, rk, rv, state0):
    R, W = rq.shape
    DB = state0.shape[0]
    T = R // DB
    consts = _ret_consts(min(RET_CHUNK, T))
    row = pl.BlockSpec((T, W), lambda b: (b, 0))
    st = pl.BlockSpec((None, RET_HEADS, RET_DIM, RET_DIM), lambda b: (b, 0, 0, 0))
    return pl.pallas_call(
        _ret_sample_kernel, grid=(DB,),
        in_specs=[row, row, row, st] + [_full(c.shape) for c in consts],
        out_specs=[row, st],
        out_shape=[jax.ShapeDtypeStruct((R, W), F32), jax.ShapeDtypeStruct(state0.shape, F32)],
        compiler_params=_params(("parallel",)), name="ret_sample",
    )(rq, rk, rv, state0, *consts)


def _softmax_rows(s):
    e = jnp.exp(s - jnp.max(s, axis=1, keepdims=True))
    return e / jnp.sum(e, axis=1, keepdims=True)


def _mem_prompt_kernel(q_ref, k_ref, v_ref, o_ref):
    q, k, v = q_ref[...], k_ref[...], v_ref[...]
    outs = []
    for h in range(MEM_HEADS):
        sl = slice(h * MEM_DIM, (h + 1) * MEM_DIM)
        s = lax.dot_general(q[:, sl].astype(BF16), k[:, sl].astype(BF16), NT,
                            preferred_element_type=F32) * (MEM_DIM ** -0.5)
        p = _softmax_rows(s).astype(BF16)
        outs.append(jnp.dot(p, v[:, sl].astype(BF16), preferred_element_type=F32))
    o_ref[...] = jnp.concatenate(outs, axis=1).astype(o_ref.dtype)


def _mem_prompt(mq, mk, mv, tm):
    S, W = mq.shape
    row = pl.BlockSpec((tm, W), lambda i: (i, 0))
    return pl.pallas_call(
        _mem_prompt_kernel, grid=(S // tm,),
        in_specs=[row, _full(mk.shape), _full(mv.shape)],
        out_specs=row, out_shape=jax.ShapeDtypeStruct((S, W), BF16),
        compiler_params=_params(("parallel",)), name="mem_prompt",
    )(mq, mk, mv)


def _mem_sample_kernel(q_ref, k_ref, v_ref, mask_ref, o_ref):
    q = q_ref[...]
    T = q.shape[0]
    qm = jnp.concatenate([q[:, h * MEM_DIM:(h + 1) * MEM_DIM] for h in range(MEM_HEADS)], axis=0).astype(BF16)
    s = lax.dot_general(qm, k_ref[...].astype(BF16), NT, preferred_element_type=F32) * (MEM_DIM ** -0.5)
    p = _softmax_rows(s + mask_ref[...]).astype(BF16)
    o = jnp.dot(p, v_ref[...].astype(BF16), preferred_element_type=F32)
    o_ref[...] = jnp.concatenate([o[h * T:(h + 1) * T, :] for h in range(MEM_HEADS)], axis=1)


def _mem_sample(mq, cache_k, cache_v):
    R, W = mq.shape
    DB, M, H, d = cache_k.shape
    T = R // DB
    kf = cache_k.reshape(DB, M * H, d)
    vf = cache_v.reshape(DB, M * H, d)
    mask = np.where((np.arange(M * H) % H)[None, :] == (np.arange(H * T) // T)[:, None], 0.0, NEG)
    mask = jnp.asarray(mask, F32)
    row = pl.BlockSpec((T, W), lambda b: (b, 0))
    kv = pl.BlockSpec((None, M * H, d), lambda b: (b, 0, 0))
    return pl.pallas_call(
        _mem_sample_kernel, grid=(DB,),
        in_specs=[row, kv, kv, _full(mask.shape)],
        out_specs=row, out_shape=jax.ShapeDtypeStruct((R, W), F32),
        compiler_params=_params(("parallel",)), name="mem_sample",
    )(mq, kf, vf, mask)


def _mixer_out_kernel(x_ref, fox_ref, ret_ref, rg_ref, mem_ref, gate_ref, wf_ref, wr_ref, wm_ref, wo_ref, o_ref):
    D = x_ref.shape[1]
    u_fox = jnp.dot(fox_ref[0].astype(BF16), wf_ref[0], preferred_element_type=F32)
    for h in range(1, FOX_HEADS):
        u_fox += jnp.dot(fox_ref[h].astype(BF16), wf_ref[h], preferred_element_type=F32)
    ret, rg = ret_ref[...], rg_ref[...]
    ys = []
    for h in range(RET_HEADS):
        sl = slice(h * RET_DIM, (h + 1) * RET_DIM)
        rf = ret[:, sl]
        rn = rf * lax.rsqrt(jnp.mean(rf * rf, axis=1, keepdims=True) + EPS)
        g = rg[:, sl]
        ys.append((rn * (g * _sigmoid(g))).astype(BF16))
    u_ret = jnp.dot(jnp.concatenate(ys, axis=1), wr_ref[...], preferred_element_type=F32)
    u_mem = jnp.dot(mem_ref[...].astype(BF16), wm_ref[...], preferred_element_type=F32)
    gates = gate_ref[...]
    merged = (_sigmoid(gates[:, :D]) * u_fox + _sigmoid(gates[:, D:2 * D]) * u_ret
              + _sigmoid(gates[:, 2 * D:]) * u_mem)
    o_ref[...] = x_ref[...] + jnp.dot(merged.astype(BF16), wo_ref[...], preferred_element_type=F32)


def _mixer_out(x, fox_o, ret_o, rg, mem_o, gates, w_up_fox, w_up_ret, w_up_mem, w_out, tm):
    R, D = x.shape
    row = lambda n: pl.BlockSpec((tm, n), lambda i: (i, 0))
    return pl.pallas_call(
        _mixer_out_kernel, grid=(R // tm,),
        in_specs=[row(D), pl.BlockSpec((FOX_HEADS, tm, FOX_DIM), lambda i: (0, i, 0)),
                  row(ret_o.shape[1]), row(rg.shape[1]), row(mem_o.shape[1]), row(gates.shape[1]),
                  _full(w_up_fox.shape), _full(w_up_ret.shape), _full(w_up_mem.shape), _full(w_out.shape)],
        out_specs=row(D), out_shape=jax.ShapeDtypeStruct((R, D), F32),
        compiler_params=_params(("parallel",)), name="mixer_out",
    )(x, fox_o, ret_o, rg, mem_o, gates, w_up_fox, w_up_ret, w_up_mem, w_out)


def _router_kernel(h_ref, g_ref, wr_ref, br_ref, tri_ref, hn_ref, gate_ref, rank_ref, cnt_ref):
    hn = _rms(h_ref[...], g_ref[...])
    hn_ref[...] = hn.astype(BF16)
    logits = lax.dot_general(wr_ref[...], hn, NT, preferred_element_type=F32, precision=HI) + br_ref[...]
    E, TS = logits.shape
    e_iota = lax.broadcasted_iota(I32, (E, TS), 0)
    work = logits
    vals, hots = [], []
    for _ in range(TOP_K):
        mx = jnp.max(work, axis=0, keepdims=True)
        idx = jnp.min(jnp.where(work == mx, e_iota, E), axis=0, keepdims=True)
        hot = e_iota == idx
        vals.append(mx)
        hots.append(hot)
        work = jnp.where(hot, -jnp.inf, work)
    es = [jnp.exp(v - vals[0]) for v in vals]
    den = es[0]
    for e in es[1:]:
        den = den + e
    gate = jnp.zeros((E, TS), F32)
    sel = jnp.zeros((E, TS), F32)
    for hot, e in zip(hots, es):
        gate = jnp.where(hot, e / den, gate)
        sel = jnp.where(hot, 1.0, sel)
    cum = jnp.dot(sel.astype(BF16), tri_ref[...], preferred_element_type=F32)
    gate_ref[...] = gate
    rank_ref[...] = jnp.where(sel > 0.0, cum - 1.0, -1.0).astype(I32)
    cnt_ref[...] = jnp.broadcast_to(cum[:, TS - 1:TS], cnt_ref.shape).astype(I32)


def _router(h, g, w_router_t, b_router, ts):
    T, D = h.shape
    E = w_router_t.shape[0]
    ns = T // ts
    tri = jnp.triu(jnp.ones((ts, ts), BF16))
    tile = lambda dt: jax.ShapeDtypeStruct((ns, E, ts), dt)
    tspec = pl.BlockSpec((None, E, ts), lambda i: (i, 0, 0))
    return pl.pallas_call(
        _router_kernel, grid=(ns,),
        in_specs=[pl.BlockSpec((ts, D), lambda i: (i, 0)), _full((1, D)), _full((E, D)), _full((E, 1)),
                  _full((ts, ts))],
        out_specs=[pl.BlockSpec((ts, D), lambda i: (i, 0)), tspec, tspec,
                   pl.BlockSpec((None, E, 128), lambda i: (i, 0, 0))],
        out_shape=[jax.ShapeDtypeStruct((T, D), BF16), tile(F32), tile(I32),
                   jax.ShapeDtypeStruct((ns, E, 128), I32)],
        compiler_params=_params(("parallel",)), name="router",
    )(h, g, w_router_t, b_router, tri)


def _moe_kernel(cnt_ref, x_ref, gate_ref, rank_ref, w1_ref, b1_ref, w2_ref, b2_ref, o_ref, *, ns, ts, rb):
    g, e, si = pl.program_id(0), pl.program_id(1), pl.program_id(2)

    @pl.when((e == 0) & (si == 0))
    def _():
        o_ref[...] = jnp.zeros_like(o_ref)

    E = pl.num_programs(1)
    cnt = cnt_ref[(g * ns + si) * E + e]
    F = w2_ref.shape[0]
    t0 = pl.multiple_of(si * ts, ts)

    def block(bi, carry):
        rank = rank_ref[...]
        rows = bi * rb + lax.broadcasted_iota(I32, (rb, ts), 0)
        hit = rows == rank
        onehot = jnp.where(hit, 1.0, 0.0).astype(BF16)
        xb = jnp.dot(onehot, x_ref[pl.ds(t0, ts), :], preferred_element_type=F32).astype(BF16)
        hdn = jnp.dot(xb, w1_ref[...], preferred_element_type=F32) + b1_ref[...]
        glu = jnp.minimum(hdn[:, :F], SWIGLU_LIMIT)
        lin = jnp.clip(hdn[:, F:], -SWIGLU_LIMIT, SWIGLU_LIMIT)
        act = (glu * _sigmoid(SWIGLU_ALPHA * glu) * (lin + 1.0)).astype(BF16)
        y = jnp.dot(act, w2_ref[...], preferred_element_type=F32) + b2_ref[...]
        wrow = jnp.sum(jnp.where(hit, gate_ref[...], 0.0), axis=1, keepdims=True)
        yw = (y * wrow).astype(BF16)
        o_ref[pl.ds(t0, ts), :] += lax.dot_general(onehot, yw, TN, preferred_element_type=F32)
        return carry

    lax.fori_loop(0, (cnt + rb - 1) // rb, block, 0)


def _moe(hn, gate, rank, counts, w1, b1, w2, b2, ts, ns, rb):
    T, D = hn.shape
    E, _, F2 = w1.shape
    ngroups = T // (ts * ns)
    tile = lambda: pl.BlockSpec((None, None, 1, ts), lambda g, e, s, c: (g * ns + s, e, 0, 0))
    grid_spec = pltpu.PrefetchScalarGridSpec(
        num_scalar_prefetch=1, grid=(ngroups, E, ns),
        in_specs=[pl.BlockSpec((ns * ts, D), lambda g, e, s, c: (g, 0)), tile(), tile(),
                  pl.BlockSpec((None, D, F2), lambda g, e, s, c: (e, 0, 0)),
                  pl.BlockSpec((None, 1, F2), lambda g, e, s, c: (e, 0, 0)),
                  pl.BlockSpec((None, F2 // 2, D), lambda g, e, s, c: (e, 0, 0)),
                  pl.BlockSpec((None, 1, D), lambda g, e, s, c: (e, 0, 0))],
        out_specs=pl.BlockSpec((ns * ts, D), lambda g, e, s, c: (g, 0)))
    nst = T // ts
    return pl.pallas_call(
        functools.partial(_moe_kernel, ns=ns, ts=ts, rb=rb), grid_spec=grid_spec,
        out_shape=jax.ShapeDtypeStruct((T, D), F32),
        compiler_params=_params(("parallel", "arbitrary", "arbitrary")), name="moe",
    )(counts, hn, gate.reshape(nst, E, 1, ts), rank.reshape(nst, E, 1, ts), w1, b1, w2, b2)


def _final_kernel(h_ref, y_ref, g_ref, o_ref):
    o_ref[...] = _rms(h_ref[...] + y_ref[...], g_ref[...])


def _final_norm(h, y, g, tm):
    R, D = h.shape
    row = pl.BlockSpec((tm, D), lambda i: (i, 0))
    return pl.pallas_call(
        _final_kernel, grid=(R // tm,), in_specs=[row, row, _full((1, D))], out_specs=row,
        out_shape=jax.ShapeDtypeStruct((R, D), F32),
        compiler_params=_params(("parallel",)), name="final_norm",
    )(h, y, g)


def _rope_tables(pos):
    half = RET_DIM // 2
    inv_freq = 1.0 / (ROPE_BASE ** jnp.linspace(0.0, 1.0, half, dtype=F32))
    ang = pos.astype(F32)[:, None] * inv_freq[None, :]
    cos, sin = jnp.cos(ang), jnp.sin(ang)
    return jnp.concatenate([cos, cos], axis=1), jnp.concatenate([-sin, sin], axis=1)


def _tile_of(n, pref):
    t = min(n, pref)
    assert n % t == 0, (n, t)
    return t


def _moe_layer(h, lw, ts, ns, rb):
    g_moe, w_router_t, b_router, w1, b1, w2, b2 = lw
    hn, gate, rank, cnt = _router(h, g_moe, w_router_t, b_router, ts)
    counts = cnt[:, :, 0].reshape(-1)
    return _moe(hn, gate, rank, counts, w1, b1, w2, b2, ts, ns, rb)


def kernel(x_prompt, x_sample, mem_prompt, cache_fox_k, cache_fox_v, cache_fox_logf, state_ret, cache_mem_k, cache_mem_v, page_table, g_mix, w_in, b_forget, w_up_fox, w_up_ret, w_up_mem, w_out, g_mem, w_mem_kv, g_moe, w_router, b_router, w_ff1, b_ff1, w_ff2, b_ff2, g_final):
    B, S, D = x_prompt.shape
    DB, T, _ = x_sample.shape
    depth = g_mix.shape[0]
    assert B == 1 and depth == 1
    NP = page_table.shape[1]
    page = cache_fox_k.shape[2]
    past_len = NP * page
    fw, rw, mw = FOX_HEADS * FOX_DIM, RET_HEADS * RET_DIM, MEM_HEADS * MEM_DIM
    R = DB * T
    l = 0

    wi = w_in[l]
    c0 = 3 * fw
    c1 = c0 + FOX_HEADS
    c2 = c1 + 4 * rw
    c3 = c2 + mw
    w_ffp = jnp.zeros((D, 128), F32).at[:, :FOX_HEADS].set(wi[:, c0:c1])
    wparts = tuple(t.astype(BF16) for t in (wi[:, :c0], w_ffp, wi[:, c1:c2], wi[:, c2:c3], wi[:, c3:]))
    bfp = jnp.zeros((1, 128), F32).at[0, :FOX_HEADS].set(b_forget[l])
    gm = g_mix[l].reshape(1, D)
    wuf = w_up_fox[l].astype(BF16).reshape(FOX_HEADS, FOX_DIM, D)
    wur, wum, wo = (t[l].astype(BF16) for t in (w_up_ret, w_up_mem, w_out))
    moe_w = (g_moe[l].reshape(1, D), jnp.swapaxes(w_router[l], 0, 1), b_router[l].reshape(-1, 1),
             w_ff1[l].astype(BF16), b_ff1[l][:, None, :], w_ff2[l].astype(BF16), b_ff2[l][:, None, :])
    gf = g_final.reshape(1, D)

    xp = x_prompt.reshape(S, D)
    tmp = _tile_of(S, 512)
    cos_p, sin_p = _rope_tables(jnp.arange(S))
    mk, mv = _memory_kv(mem_prompt[0], g_mem[l].reshape(1, D), w_mem_kv[l].astype(BF16))
    fq, fk, fv, lf, rq, rk, rv, rg, mq, gates = _input_projection(xp, gm, wparts, bfp, cos_p, sin_p, tmp)
    qa, ka, vh = _fox_prep(fq, fk, fv, lf, tmp)
    tk = _tile_of(S, 1024)
    fox_o = _fox_prompt(qa, ka, vh, _tile_of(tk, 512), tk)
    ret_o, rs_p = _ret_prompt(rq, rk, rv)
    mem_o = _mem_prompt(mq, mk, mv, tmp)
    hp = _mixer_out(xp, fox_o, ret_o, rg, mem_o, gates, wuf, wur, wum, wo, _tile_of(S, 256))
    ts = _tile_of(S, 1024)
    yp = _moe_layer(hp, moe_w, ts, _tile_of(S // ts, 2), 128)
    y_prompt = _final_norm(hp, yp, gf, tmp)

    xs = x_sample.reshape(R, D)
    tms = _tile_of(R, 512)
    cos_s, sin_s = _rope_tables(past_len + jnp.arange(T))
    cos_s, sin_s = jnp.tile(cos_s, (DB, 1)), jnp.tile(sin_s, (DB, 1))
    fq_s, fk_s, fv_s, lf_s, rq_s, rk_s, rv_s, rg_s, mq_s, gates_s = _input_projection(
        xs, gm, wparts, bfp, cos_s, sin_s, tms)
    fox_os = _fox_sample(fq_s, fk_s, fv_s, lf_s, page_table, cache_fox_k[l], cache_fox_v[l], cache_fox_logf[l],
                         _tile_of(NP, 4))
    ret_os, rs_s = _ret_sample(rq_s, rk_s, rv_s, state_ret[l])
    mem_os = _mem_sample(mq_s, cache_mem_k[l], cache_mem_v[l])
    hs = _mixer_out(xs, fox_os, ret_os, rg_s, mem_os, gates_s, wuf, wur, wum, wo, _tile_of(R, 256))
    tss = _tile_of(R, 1024)
    ys = _moe_layer(hs, moe_w, tss, 1, 128)
    y_sample = _final_norm(hs, ys, gf, tms)

    return (y_prompt.reshape(B, S, D), y_sample.reshape(DB, T, D),
            fk.reshape(1, B, S, FOX_HEADS, FOX_DIM), fv.reshape(1, B, S, FOX_HEADS, FOX_DIM),
            lf.reshape(1, B, S, FOX_HEADS), rs_p.reshape(1, B, RET_HEADS, RET_DIM, RET_DIM),
            mk.reshape(1, B, -1, MEM_HEADS, MEM_DIM), mv.reshape(1, B, -1, MEM_HEADS, MEM_DIM),
            fk_s.reshape(1, DB, T, FOX_HEADS, FOX_DIM), fv_s.reshape(1, DB, T, FOX_HEADS, FOX_DIM),
            lf_s.reshape(1, DB, T, FOX_HEADS), rs_s.reshape(1, DB, RET_HEADS, RET_DIM, RET_DIM))
```

```python
import functools
import math

import jax
import jax.numpy as jnp
import numpy as np
from jax import lax
from jax.experimental import pallas as pl
from jax.experimental.pallas import tpu as pltpu

F32 = jnp.float32
BF16 = jnp.bfloat16
I32 = jnp.int32

FOX_HEADS = 8
FOX_DIM = 64
RET_HEADS = 4
RET_DIM = 128
RET_CHUNK = 128
MEM_HEADS = 4
MEM_DIM = 128
N_BRANCH = 3
TOP_K = 4
ROPE_BASE = 10000.0
SWIGLU_LIMIT = 7.0
SWIGLU_ALPHA = 1.702
EPS = 1e-6

LOG2E = 1.4426950408889634
NEG = -1e30
VMEM_LIMIT = 48 * 1024 * 1024

HI = lax.Precision.HIGHEST
NT = (((1,), (1,)), ((), ()))
TN = (((0,), (0,)), ((), ()))


def _params(sem, vmem=VMEM_LIMIT):
    return pltpu.CompilerParams(dimension_semantics=sem, vmem_limit_bytes=vmem)


def _rms(x, g):
    return x * lax.rsqrt(jnp.mean(x * x, axis=-1, keepdims=True) + EPS) * g


def _log_sigmoid(x):
    return jnp.minimum(x, 0.0) - jnp.log(1.0 + jnp.exp(-jnp.abs(x)))


def _sigmoid(x):
    return 1.0 / (1.0 + jnp.exp(-x))


def _split3(x):
    hi = x.astype(BF16)
    r = x - hi.astype(F32)
    mid = r.astype(BF16)
    lo = (r - mid.astype(F32)).astype(BF16)
    return hi, mid, lo


def _full(shape):
    nd = len(shape)
    return pl.BlockSpec(shape, lambda *_: (0,) * nd)


def _proj_fox_kernel(x_ref, g_ref, w_ref, wf_ref, bf_ref, fq_ref, fk_ref, fv_ref, lf_ref):
    xn = _rms(x_ref[...], g_ref[...]).astype(BF16)
    y = jnp.dot(xn, w_ref[...], preferred_element_type=F32)
    w = fq_ref.shape[1]
    fq_ref[...] = y[:, :w]
    fk_ref[...] = y[:, w:2 * w]
    fv_ref[...] = y[:, 2 * w:]
    ff = jnp.dot(xn, wf_ref[...], preferred_element_type=F32)
    lf_ref[...] = _log_sigmoid(ff + bf_ref[...])[:, :lf_ref.shape[1]]


def _rotate(y, cos, sin):
    outs = []
    for h in range(RET_HEADS):
        yh = y[:, h * RET_DIM:(h + 1) * RET_DIM]
        outs.append(yh * cos + pltpu.roll(yh, RET_DIM // 2, axis=1) * sin)
    return jnp.concatenate(outs, axis=1)


def _proj_ret_kernel(x_ref, g_ref, w_ref, wm_ref, cos_ref, sin_ref, rq_ref, rk_ref, rv_ref, rg_ref, mq_ref):
    xn = _rms(x_ref[...], g_ref[...]).astype(BF16)
    y = jnp.dot(xn, w_ref[...], preferred_element_type=F32)
    w = rq_ref.shape[1]
    cos, sin = cos_ref[...], sin_ref[...]
    rq_ref[...] = _rotate(y[:, :w], cos, sin)
    rk_ref[...] = _rotate(y[:, w:2 * w], cos, sin) * (RET_DIM ** -0.5)
    rv_ref[...] = y[:, 2 * w:3 * w]
    rg_ref[...] = y[:, 3 * w:]
    mq_ref[...] = jnp.dot(xn, wm_ref[...], preferred_element_type=F32)


def _proj_gate_kernel(x_ref, g_ref, w_ref, o_ref):
    xn = _rms(x_ref[...], g_ref[...]).astype(BF16)
    o_ref[...] = jnp.dot(xn, w_ref[...], preferred_element_type=F32)


def _input_projection(x, g, wparts, b_forget_pad, cos, sin, tm):
    R, D = x.shape
    w_fox, w_ff, w_ret, w_mem, w_gate = wparts
    fw = FOX_HEADS * FOX_DIM
    rw = RET_HEADS * RET_DIM
    mw = MEM_HEADS * MEM_DIM
    row = lambda n: pl.BlockSpec((tm, n), lambda i: (i, 0))
    sds = lambda n: jax.ShapeDtypeStruct((R, n), F32)
    grid = (R // tm,)
    fq, fk, fv, lf = pl.pallas_call(
        _proj_fox_kernel, grid=grid,
        in_specs=[row(D), _full((1, D)), _full(w_fox.shape), _full(w_ff.shape), _full((1, 128))],
        out_specs=[row(fw), row(fw), row(fw), row(FOX_HEADS)],
        out_shape=[sds(fw), sds(fw), sds(fw), sds(FOX_HEADS)],
        compiler_params=_params(("parallel",)), name="proj_fox",
    )(x, g, w_fox, w_ff, b_forget_pad)
    rq, rk, rv, rg, mq = pl.pallas_call(
        _proj_ret_kernel, grid=grid,
        in_specs=[row(D), _full((1, D)), _full(w_ret.shape), _full(w_mem.shape), row(RET_DIM), row(RET_DIM)],
        out_specs=[row(rw), row(rw), row(rw), row(rw), row(mw)],
        out_shape=[sds(rw), sds(rw), sds(rw), sds(rw), sds(mw)],
        compiler_params=_params(("parallel",)), name="proj_ret",
    )(x, g, w_ret, w_mem, cos, sin)
    tg = min(tm, 256)
    gates = pl.pallas_call(
        _proj_gate_kernel, grid=(R // tg,),
        in_specs=[pl.BlockSpec((tg, D), lambda i: (i, 0)), _full((1, D)), _full(w_gate.shape)],
        out_specs=pl.BlockSpec((tg, w_gate.shape[1]), lambda i: (i, 0)),
        out_shape=sds(w_gate.shape[1]),
        compiler_params=_params(("parallel",)), name="proj_gate",
    )(x, g, w_gate)
    return fq, fk, fv, lf, rq, rk, rv, rg, mq, gates


def _memkv_kernel(x_ref, g_ref, w_ref, mk_ref, mv_ref):
    xn = _rms(x_ref[...], g_ref[...]).astype(BF16)
    y = jnp.dot(xn, w_ref[...], preferred_element_type=F32)
    w = mk_ref.shape[1]
    mk_ref[...] = y[:, :w]
    mv_ref[...] = y[:, w:]


def _memory_kv(mem, g, w):
    M, D = mem.shape
    mw = MEM_HEADS * MEM_DIM
    return pl.pallas_call(
        _memkv_kernel, grid=(1,),
        in_specs=[_full((M, D)), _full((1, D)), _full(w.shape)],
        out_specs=[_full((M, mw)), _full((M, mw))],
        out_shape=[jax.ShapeDtypeStruct((M, mw), F32)] * 2,
        compiler_params=_params(("arbitrary",)), name="memory_kv",
    )(mem, g, w)


def _fox_prep_kernel(fq_ref, fk_ref, fv_ref, lf_ref, tri_ref, qa_ref, ka_ref, v_ref, carry_ref):
    @pl.when(pl.program_id(0) == 0)
    def _():
        carry_ref[...] = jnp.zeros_like(carry_ref)

    tm = fq_ref.shape[0]
    c = jnp.dot(tri_ref[...], lf_ref[...], preferred_element_type=F32, precision=HI) + carry_ref[...]
    carry_ref[...] = c[tm - 1:tm, :]
    hi, mid, lo = (p.astype(F32) for p in _split3(c * LOG2E))
    lane = lax.broadcasted_iota(I32, (tm, 2 * FOX_DIM), 1)
    d = FOX_DIM
    fq, fk, fv = fq_ref[...], fk_ref[...], fv_ref[...]
    for h in range(FOX_HEADS):
        g = h // 2
        sl = slice(g * 2 * d, (g + 1) * 2 * d)
        qg, kg, vg = fq[:, sl], fk[:, sl], fv[:, sl]
        if h % 2:
            qg, kg, vg = (pltpu.roll(t, d, axis=1) for t in (qg, kg, vg))
        ch, cm, cl = hi[:, h:h + 1], mid[:, h:h + 1], lo[:, h:h + 1]
        one = jnp.where((lane >= d + 3) & (lane < d + 6), 1.0, 0.0)
        qa = jnp.where(lane < d, qg * (d ** -0.5 * LOG2E),
                       jnp.where(lane == d, ch, jnp.where(lane == d + 1, cm, jnp.where(lane == d + 2, cl, one))))
        onek = jnp.where((lane >= d) & (lane < d + 3), 1.0, 0.0)
        ka = jnp.where(lane < d, kg,
                       jnp.where(lane == d + 3, -ch, jnp.where(lane == d + 4, -cm, jnp.where(lane == d + 5, -cl, onek))))
        qa_ref[h] = qa.astype(BF16)
        ka_ref[h] = ka.astype(BF16)
        v_ref[h] = vg[:, :d].astype(BF16)


def _fox_prep(fq, fk, fv, lf, tm):
    S, W = fq.shape
    tri = jnp.tril(jnp.ones((tm, tm), F32))
    row = lambda n: pl.BlockSpec((tm, n), lambda i: (i, 0))
    hm = lambda n: pl.BlockSpec((FOX_HEADS, tm, n), lambda i: (0, i, 0))
    return pl.pallas_call(
        _fox_prep_kernel, grid=(S // tm,),
        in_specs=[row(W), row(W), row(W), row(FOX_HEADS), _full((tm, tm))],
        out_specs=[hm(2 * FOX_DIM), hm(2 * FOX_DIM), hm(FOX_DIM)],
        out_shape=[jax.ShapeDtypeStruct((FOX_HEADS, S, 2 * FOX_DIM), BF16)] * 2
        + [jax.ShapeDtypeStruct((FOX_HEADS, S, FOX_DIM), BF16)],
        scratch_shapes=[pltpu.VMEM((1, FOX_HEADS), F32)],
        compiler_params=_params(("arbitrary",)), name="fox_prep",
    )(fq, fk, fv, lf, tri)


def _fox_prompt_kernel(q_ref, k_ref, v_ref, o_ref, *, tq, tk):
    i = pl.program_id(1)
    q = q_ref[...]
    nfull = (i * tq) // tk

    def step(j, carry, masked):
        m, l, acc = carry
        k0 = pl.multiple_of(j * tk, tk)
        k = k_ref[pl.ds(k0, tk), :]
        v = v_ref[pl.ds(k0, tk), :]
        s = lax.dot_general(q, k, NT, preferred_element_type=F32)
        if masked:
            rowg = i * tq + lax.broadcasted_iota(I32, (tq, tk), 0)
            colg = k0 + lax.broadcasted_iota(I32, (tq, tk), 1)
            s = jnp.where(colg <= rowg, s, NEG)
        m_new = jnp.maximum(m, jnp.max(s, axis=1, keepdims=True))
        alpha = jnp.exp2(m - m_new)
        p = jnp.exp2(s - m_new)
        l = alpha * l + jnp.sum(p, axis=1, keepdims=True)
        acc = alpha * acc + jnp.dot(p.astype(BF16), v, preferred_element_type=F32)
        return m_new, l, acc

    init = (jnp.full((tq, 1), NEG, F32), jnp.zeros((tq, 1), F32), jnp.zeros((tq, FOX_DIM), F32))
    carry = lax.fori_loop(0, nfull, lambda j, c: step(j, c, False), init)
    m, l, acc = step(nfull, carry, True)
    o_ref[...] = (acc / l).astype(o_ref.dtype)


def _fox_prompt(qa, ka, v, tq, tk):
    H, S, A = qa.shape
    return pl.pallas_call(
        functools.partial(_fox_prompt_kernel, tq=tq, tk=tk), grid=(H, S // tq),
        in_specs=[pl.BlockSpec((None, tq, A), lambda h, i: (h, i, 0)),
                  pl.BlockSpec((None, S, A), lambda h, i: (h, 0, 0)),
                  pl.BlockSpec((None, S, FOX_DIM), lambda h, i: (h, 0, 0))],
        out_specs=pl.BlockSpec((None, tq, FOX_DIM), lambda h, i: (h, i, 0)),
        out_shape=jax.ShapeDtypeStruct((H, S, FOX_DIM), BF16),
        compiler_params=_params(("parallel", "arbitrary")), name="fox_prompt",
    )(qa, ka, v)


def _fox_sample_kernel(pt_ref, q_ref, kn_ref, vn_ref, lfn_ref, lfnt_ref, ms_ref, *rest, pp):
    k_refs, v_refs, l_refs = rest[:pp], rest[pp:2 * pp], rest[2 * pp:3 * pp]
    o_ref = rest[3 * pp]
    m_ref, l_ref, acc_ref, carry_ref, qbd_ref, rowc_ref = rest[3 * pp + 1:]
    j = pl.program_id(1)
    H, d, T = FOX_HEADS, FOX_DIM, q_ref.shape[0]
    HT, W = H * T, H * d
    page = ms_ref.shape[0]

    @pl.when(j == 0)
    def _():
        qsc = q_ref[...] * (d ** -0.5 * LOG2E)
        lfn, lfnt = lfn_ref[...], lfnt_ref[...]
        ti = lax.broadcasted_iota(I32, (T, T), 0)
        tj = lax.broadcasted_iota(I32, (T, T), 1)
        low = jnp.where(tj <= ti, 1.0, 0.0)
        up = jnp.where(ti <= tj, 1.0, 0.0)
        cq = jnp.zeros((T, H), F32)
        cqt = jnp.zeros((H, T), F32)
        for t in range(T):
            cq = cq + low[:, t:t + 1] * lfn[t:t + 1, :]
            cqt = cqt + lfnt[:, t:t + 1] * up[t:t + 1, :]
        bias = jnp.concatenate([cq[:, h:h + 1] - cqt[h:h + 1, :] for h in range(H)], axis=0) * LOG2E
        rowc_ref[...] = jnp.concatenate([cq[:, h:h + 1] for h in range(H)], axis=0) * LOG2E
        r_i = lax.broadcasted_iota(I32, (HT, W), 0)
        c_i = lax.broadcasted_iota(I32, (HT, W), 1)
        qbd = jnp.where(c_i // d == r_i // T, jnp.concatenate([qsc] * H, axis=0), 0.0).astype(BF16)
        qbd_ref[...] = qbd
        s = lax.dot_general(qbd, kn_ref[...].astype(BF16), NT, preferred_element_type=F32) + bias
        rr = lax.broadcasted_iota(I32, (HT, T), 0)
        cc = lax.broadcasted_iota(I32, (HT, T), 1)
        s = jnp.where(cc <= rr % T, s, NEG)
        m0 = jnp.max(s, axis=1, keepdims=True)
        p = jnp.exp2(s - m0)
        m_ref[...] = m0
        l_ref[...] = jnp.sum(p, axis=1, keepdims=True)
        acc_ref[...] = jnp.dot(p.astype(BF16), vn_ref[...].astype(BF16), preferred_element_type=F32)
        carry_ref[...] = jnp.zeros_like(carry_ref)

    qbd = qbd_ref[...]
    rowc = rowc_ref[...]
    ms = ms_ref[...]
    lf = jnp.concatenate([l_refs[r][...] for r in range(pp)], axis=0)
    pieces = jnp.concatenate([pc.astype(F32) for pc in _split3(lf)], axis=0).astype(BF16)
    rr3 = jnp.dot(pieces, ms, preferred_element_type=F32)
    n = pp * H
    rr = rr3[:n] + rr3[n:2 * n] + rr3[2 * n:]
    carry = carry_ref[...]
    biases = []
    for r in range(pp):
        rs = (rr[r * H:(r + 1) * H, :page] + carry[:, 0:1]) * LOG2E
        carry = carry + rr[r * H:(r + 1) * H, page:]
        biases.append(jnp.concatenate([jnp.broadcast_to(rs[h:h + 1, :], (T, page)) for h in range(H)], axis=0))
    carry_ref[...] = carry
    bias = jnp.concatenate(biases, axis=1) + rowc
    kb = jnp.concatenate([k_refs[r][...] for r in range(pp)], axis=1).astype(BF16)
    vb = jnp.concatenate([v_refs[r][...] for r in range(pp)], axis=1).astype(BF16)
    s = jnp.dot(qbd, kb, preferred_element_type=F32) + bias
    m = m_ref[...]
    m_new = jnp.maximum(m, jnp.max(s, axis=1, keepdims=True))
    alpha = jnp.exp2(m - m_new)
    p = jnp.exp2(s - m_new)
    l_ref[...] = alpha * l_ref[...] + jnp.sum(p, axis=1, keepdims=True)
    acc_ref[...] = alpha * acc_ref[...] + lax.dot_general(p.astype(BF16), vb, NT, preferred_element_type=F32)
    m_ref[...] = m_new

    @pl.when(j == pl.num_programs(1) - 1)
    def _():
        acc = acc_ref[...]
        o = jnp.concatenate([acc[h * T:(h + 1) * T, h * d:(h + 1) * d] for h in range(H)], axis=0)
        o_ref[...] = (o / l_ref[...]).reshape(H, T, d)


def _fox_sample(q, kn, vn, lfn, page_table, cache_k, cache_v, cache_lf, pp):
    R, W = q.shape
    DB, NP = page_table.shape
    T = R // DB
    n_pool, page, H, d = cache_k.shape
    ck = jnp.transpose(cache_k, (0, 2, 3, 1)).reshape(n_pool, H * d, page)
    cv = jnp.transpose(cache_v, (0, 2, 3, 1)).reshape(n_pool, H * d, page)
    cl = jnp.transpose(cache_lf, (0, 2, 1))
    lfnt = jnp.swapaxes(lfn.reshape(DB, T, H), 1, 2)
    tt = np.arange(page)
    ms = np.concatenate([tt[:, None] > tt[None, :], np.ones((page, 128), bool)], axis=1)
    ms = jnp.asarray(ms, BF16)
    pt = page_table.reshape(-1).astype(I32)

    def page_spec(shape, r):
        return pl.BlockSpec((None,) + shape, lambda b, j, pt: (pt[b * NP + NP - 1 - (j * pp + r)], 0, 0))

    tok = lambda n: pl.BlockSpec((T, n), lambda b, j, pt: (b, 0))
    in_specs = [tok(W), tok(W), tok(W), tok(H), pl.BlockSpec((None, H, T), lambda b, j, pt: (b, 0, 0)),
                pl.BlockSpec(ms.shape, lambda b, j, pt: (0, 0))]
    in_specs += [page_spec((H * d, page), r) for r in range(pp)]
    in_specs += [page_spec((H * d, page), r) for r in range(pp)]
    in_specs += [page_spec((H, page), r) for r in range(pp)]
    grid_spec = pltpu.PrefetchScalarGridSpec(
        num_scalar_prefetch=1, grid=(DB, NP // pp), in_specs=in_specs,
        out_specs=pl.BlockSpec((H, T, d), lambda b, j, pt: (0, b, 0)),
        scratch_shapes=[pltpu.VMEM((H * T, 1), F32), pltpu.VMEM((H * T, 1), F32), pltpu.VMEM((H * T, W), F32),
                        pltpu.VMEM((H, 128), F32), pltpu.VMEM((H * T, W), BF16), pltpu.VMEM((H * T, 1), F32)])
    return pl.pallas_call(
        functools.partial(_fox_sample_kernel, pp=pp), grid_spec=grid_spec,
        out_shape=jax.ShapeDtypeStruct((H, R, d), F32),
        compiler_params=_params(("parallel", "arbitrary")), name="fox_sample",
    )(pt, q, kn, vn, lfn, lfnt, ms, *([ck] * pp), *([cv] * pp), *([cl] * pp))


def _ret_consts(C):
    lg = jnp.log(1.0 - jnp.exp2(-5.0 - jnp.arange(RET_HEADS, dtype=F32)))
    idx = jnp.arange(C, dtype=F32)
    rel = idx[:, None] - idx[None, :]
    intra = jnp.where(rel >= 0, jnp.exp(jnp.maximum(rel, 0.0)[None] * lg[:, None, None]), 0.0)
    qd = jnp.exp((idx + 1.0)[None, :] * lg[:, None])
    kd = jnp.exp((C - 1.0 - idx)[None, :] * lg[:, None])
    cd = jnp.exp(C * lg)
    rep = lambda t: jnp.broadcast_to(t[:, :, None], t.shape + (RET_DIM,))
    return intra, rep(qd), rep(kd), jnp.broadcast_to(cd[:, None, None], (RET_HEADS, 1, RET_DIM))


def _ret_heads(q, k, v, state_of, intra_ref, qd_ref, kd_ref, cd_ref):
    outs, states = [], []
    for h in range(RET_HEADS):
        sl = slice(h * RET_DIM, (h + 1) * RET_DIM)
        qh, kh, vh = q[:, sl].astype(BF16), k[:, sl], v[:, sl].astype(BF16)
        st = state_of(h)
        a = lax.dot_general(qh, kh.astype(BF16), NT, preferred_element_type=F32) * intra_ref[h]
        o = (jnp.dot(a.astype(BF16), vh, preferred_element_type=F32)
             + jnp.dot(qh, st.astype(BF16), preferred_element_type=F32) * qd_ref[h])
        kdec = (kh * kd_ref[h]).astype(BF16)
        states.append(cd_ref[h] * st + lax.dot_general(kdec, vh, TN, preferred_element_type=F32))
        outs.append(o)
    return jnp.concatenate(outs, axis=1), states


def _ret_prompt_kernel(q_ref, k_ref, v_ref, intra_ref, qd_ref, kd_ref, cd_ref, o_ref, so_ref, st_ref):
    @pl.when(pl.program_id(0) == 0)
    def _():
        st_ref[...] = jnp.zeros_like(st_ref)

    o, states = _ret_heads(q_ref[...], k_ref[...], v_ref[...], lambda h: st_ref[h],
                           intra_ref, qd_ref, kd_ref, cd_ref)
    o_ref[...] = o
    for h in range(RET_HEADS):
        st_ref[h] = states[h]

    @pl.when(pl.program_id(0) == pl.num_programs(0) - 1)
    def _():
        so_ref[...] = st_ref[...]


def _ret_prompt(rq, rk, rv):
    S, W = rq.shape
    C = min(RET_CHUNK, S)
    consts = _ret_consts(C)
    row = pl.BlockSpec((C, W), lambda i: (i, 0))
    st_shape = (RET_HEADS, RET_DIM, RET_DIM)
    return pl.pallas_call(
        _ret_prompt_kernel, grid=(S // C,),
        in_specs=[row, row, row] + [_full(c.shape) for c in consts],
        out_specs=[row, _full(st_shape)],
        out_shape=[jax.ShapeDtypeStruct((S, W), F32), jax.ShapeDtypeStruct(st_shape, F32)],
        scratch_shapes=[pltpu.VMEM(st_shape, F32)],
        compiler_params=_params(("arbitrary",)), name="ret_prompt",
    )(rq, rk, rv, *consts)


def _ret_sample_kernel(q_ref, k_ref, v_ref, s0_ref, intra_ref, qd_ref, kd_ref, cd_ref, o_ref, so_ref):
    o, states = _ret_heads(q_ref[...], k_ref[...], v_ref[...], lambda h: s0_ref[h],
                           intra_ref, qd_ref, kd_ref, cd_ref)
    o_ref[...] = o
    for h in range(RET_HEADS):
        so_ref[h] = states[h]


def _ret_sample(rq, rk, rv, state0):
    R, W = rq.shape
    DB = state0.shape[0]
    T = R // DB
    consts = _ret_consts(min(RET_CHUNK, T))
    row = pl.BlockSpec((T, W), lambda b: (b, 0))
    st = pl.BlockSpec((None, RET_HEADS, RET_DIM, RET_DIM), lambda b: (b, 0, 0, 0))
    return pl.pallas_call(
        _ret_sample_kernel, grid=(DB,),
        in_specs=[row, row, row, st] + [_full(c.shape) for c in consts],
        out_specs=[row, st],
        out_shape=[jax.ShapeDtypeStruct((R, W), F32), jax.ShapeDtypeStruct(state0.shape, F32)],
        compiler_params=_params(("parallel",)), name="ret_sample",
    )(rq, rk, rv, state0, *consts)


def _softmax_rows(s):
    e = jnp.exp(s - jnp.max(s, axis=1, keepdims=True))
    return e / jnp.sum(e, axis=1, keepdims=True)


def _mem_prompt_kernel(q_ref, k_ref, v_ref, o_ref):
    q, k, v = q_ref[...], k_ref[...], v_ref[...]
    outs = []
    for h in range(MEM_HEADS):
        sl = slice(h * MEM_DIM, (h + 1) * MEM_DIM)
        s = lax.dot_general(q[:, sl].astype(BF16), k[:, sl].astype(BF16), NT,
                            preferred_element_type=F32) * (MEM_DIM ** -0.5)
        p = _softmax_rows(s).astype(BF16)
        outs.append(jnp.dot(p, v[:, sl].astype(BF16), preferred_element_type=F32))
    o_ref[...] = jnp.concatenate(outs, axis=1).astype(o_ref.dtype)


def _mem_prompt(mq, mk, mv, tm):
    S, W = mq.shape
    row = pl.BlockSpec((tm, W), lambda i: (i, 0))
    return pl.pallas_call(
        _mem_prompt_kernel, grid=(S // tm,),
        in_specs=[row, _full(mk.shape), _full(mv.shape)],
        out_specs=row, out_shape=jax.ShapeDtypeStruct((S, W), BF16),
        compiler_params=_params(("parallel",)), name="mem_prompt",
    )(mq, mk, mv)


def _mem_sample_kernel(q_ref, k_ref, v_ref, mask_ref, o_ref):
    q = q_ref[...]
    T = q.shape[0]
    qm = jnp.concatenate([q[:, h * MEM_DIM:(h + 1) * MEM_DIM] for h in range(MEM_HEADS)], axis=0).astype(BF16)
    s = lax.dot_general(qm, k_ref[...].astype(BF16), NT, preferred_element_type=F32) * (MEM_DIM ** -0.5)
    p = _softmax_rows(s + mask_ref[...]).astype(BF16)
    o = jnp.dot(p, v_ref[...].astype(BF16), preferred_element_type=F32)
    o_ref[...] = jnp.concatenate([o[h * T:(h + 1) * T, :] for h in range(MEM_HEADS)], axis=1)


def _mem_sample(mq, cache_k, cache_v):
    R, W = mq.shape
    DB, M, H, d = cache_k.shape
    T = R // DB
    kf = cache_k.reshape(DB, M * H, d)
    vf = cache_v.reshape(DB, M * H, d)
    mask = np.where((np.arange(M * H) % H)[None, :] == (np.arange(H * T) // T)[:, None], 0.0, NEG)
    mask = jnp.asarray(mask, F32)
    row = pl.BlockSpec((T, W), lambda b: (b, 0))
    kv = pl.BlockSpec((None, M * H, d), lambda b: (b, 0, 0))
    return pl.pallas_call(
        _mem_sample_kernel, grid=(DB,),
        in_specs=[row, kv, kv, _full(mask.shape)],
        out_specs=row, out_shape=jax.ShapeDtypeStruct((R, W), F32),
        compiler_params=_params(("parallel",)), name="mem_sample",
    )(mq, kf, vf, mask)


def _mixer_out_kernel(x_ref, fox_ref, ret_ref, rg_ref, mem_ref, gate_ref, wf_ref, wr_ref, wm_ref, wo_ref, o_ref):
    D = x_ref.shape[1]
    u_fox = jnp.dot(fox_ref[0].astype(BF16), wf_ref[0], preferred_element_type=F32)
    for h in range(1, FOX_HEADS):
        u_fox += jnp.dot(fox_ref[h].astype(BF16), wf_ref[h], preferred_element_type=F32)
    ret, rg = ret_ref[...], rg_ref[...]
    ys = []
    for h in range(RET_HEADS):
        sl = slice(h * RET_DIM, (h + 1) * RET_DIM)
        rf = ret[:, sl]
        rn = rf * lax.rsqrt(jnp.mean(rf * rf, axis=1, keepdims=True) + EPS)
        g = rg[:, sl]
        ys.append((rn * (g * _sigmoid(g))).astype(BF16))
    u_ret = jnp.dot(jnp.concatenate(ys, axis=1), wr_ref[...], preferred_element_type=F32)
    u_mem = jnp.dot(mem_ref[...].astype(BF16), wm_ref[...], preferred_element_type=F32)
    gates = gate_ref[...]
    merged = (_sigmoid(gates[:, :D]) * u_fox + _sigmoid(gates[:, D:2 * D]) * u_ret
              + _sigmoid(gates[:, 2 * D:]) * u_mem)
    o_ref[...] = x_ref[...] + jnp.dot(merged.astype(BF16), wo_ref[...], preferred_element_type=F32)


def _mixer_out(x, fox_o, ret_o, rg, mem_o, gates, w_up_fox, w_up_ret, w_up_mem, w_out, tm):
    R, D = x.shape
    row = lambda n: pl.BlockSpec((tm, n), lambda i: (i, 0))
    return pl.pallas_call(
        _mixer_out_kernel, grid=(R // tm,),
        in_specs=[row(D), pl.BlockSpec((FOX_HEADS, tm, FOX_DIM), lambda i: (0, i, 0)),
                  row(ret_o.shape[1]), row(rg.shape[1]), row(mem_o.shape[1]), row(gates.shape[1]),
                  _full(w_up_fox.shape), _full(w_up_ret.shape), _full(w_up_mem.shape), _full(w_out.shape)],
        out_specs=row(D), out_shape=jax.ShapeDtypeStruct((R, D), F32),
        compiler_params=_params(("parallel",)), name="mixer_out",
    )(x, fox_o, ret_o, rg, mem_o, gates, w_up_fox, w_up_ret, w_up_mem, w_out)


def _router_kernel(h_ref, g_ref, wr_ref, br_ref, tri_ref, hn_ref, gate_ref, rank_ref, cnt_ref):
    hn = _rms(h_ref[...], g_ref[...])
    hn_ref[...] = hn.astype(BF16)
    logits = lax.dot_general(wr_ref[...], hn, NT, preferred_element_type=F32, precision=HI) + br_ref[...]
    E, TS = logits.shape
    e_iota = lax.broadcasted_iota(I32, (E, TS), 0)
    work = logits
    vals, hots = [], []
    for _ in range(TOP_K):
        mx = jnp.max(work, axis=0, keepdims=True)
        idx = jnp.min(jnp.where(work == mx, e_iota, E), axis=0, keepdims=True)
        hot = e_iota == idx
        vals.append(mx)
        hots.append(hot)
        work = jnp.where(hot, -jnp.inf, work)
    es = [jnp.exp(v - vals[0]) for v in vals]
    den = es[0]
    for e in es[1:]:
        den = den + e
    gate = jnp.zeros((E, TS), F32)
    sel = jnp.zeros((E, TS), F32)
    for hot, e in zip(hots, es):
        gate = jnp.where(hot, e / den, gate)
        sel = jnp.where(hot, 1.0, sel)
    cum = jnp.dot(sel.astype(BF16), tri_ref[...], preferred_element_type=F32)
    gate_ref[...] = gate
    rank_ref[...] = jnp.where(sel > 0.0, cum - 1.0, -1.0).astype(I32)
    cnt_ref[...] = jnp.broadcast_to(cum[:, TS - 1:TS], cnt_ref.shape).astype(I32)


def _router(h, g, w_router_t, b_router, ts):
    T, D = h.shape
    E = w_router_t.shape[0]
    ns = T // ts
    tri = jnp.triu(jnp.ones((ts, ts), BF16))
    tile = lambda dt: jax.ShapeDtypeStruct((ns, E, ts), dt)
    tspec = pl.BlockSpec((None, E, ts), lambda i: (i, 0, 0))
    return pl.pallas_call(
        _router_kernel, grid=(ns,),
        in_specs=[pl.BlockSpec((ts, D), lambda i: (i, 0)), _full((1, D)), _full((E, D)), _full((E, 1)),
                  _full((ts, ts))],
        out_specs=[pl.BlockSpec((ts, D), lambda i: (i, 0)), tspec, tspec,
                   pl.BlockSpec((None, E, 128), lambda i: (i, 0, 0))],
        out_shape=[jax.ShapeDtypeStruct((T, D), BF16), tile(F32), tile(I32),
                   jax.ShapeDtypeStruct((ns, E, 128), I32)],
        compiler_params=_params(("parallel",)), name="router",
    )(h, g, w_router_t, b_router, tri)


def _moe_kernel(cnt_ref, x_ref, gate_ref, rank_ref, w1_ref, b1_ref, w2_ref, b2_ref, o_ref, *, ns, ts, rb):
    g, e, si = pl.program_id(0), pl.program_id(1), pl.program_id(2)

    @pl.when((e == 0) & (si == 0))
    def _():
        o_ref[...] = jnp.zeros_like(o_ref)

    E = pl.num_programs(1)
    cnt = cnt_ref[(g * ns + si) * E + e]
    F = w2_ref.shape[0]
    t0 = pl.multiple_of(si * ts, ts)

    def block(bi, carry):
        rank = rank_ref[...]
        rows = bi * rb + lax.broadcasted_iota(I32, (rb, ts), 0)
        hit = rows == rank
        onehot = jnp.where(hit, 1.0, 0.0).astype(BF16)
        xb = jnp.dot(onehot, x_ref[pl.ds(t0, ts), :], preferred_element_type=F32).astype(BF16)
        hdn = jnp.dot(xb, w1_ref[...], preferred_element_type=F32) + b1_ref[...]
        glu = jnp.minimum(hdn[:, :F], SWIGLU_LIMIT)
        lin = jnp.clip(hdn[:, F:], -SWIGLU_LIMIT, SWIGLU_LIMIT)
        act = (glu * _sigmoid(SWIGLU_ALPHA * glu) * (lin + 1.0)).astype(BF16)
        y = jnp.dot(act, w2_ref[...], preferred_element_type=F32) + b2_ref[...]
        wrow = jnp.sum(jnp.where(hit, gate_ref[...], 0.0), axis=1, keepdims=True)
        yw = (y * wrow).astype(BF16)
        o_ref[pl.ds(t0, ts), :] += lax.dot_general(onehot, yw, TN, preferred_element_type=F32)
        return carry

    lax.fori_loop(0, (cnt + rb - 1) // rb, block, 0)


def _moe(hn, gate, rank, counts, w1, b1, w2, b2, ts, ns, rb):
    T, D = hn.shape
    E, _, F2 = w1.shape
    ngroups = T // (ts * ns)
    tile = lambda: pl.BlockSpec((None, None, 1, ts), lambda g, e, s, c: (g * ns + s, e, 0, 0))
    grid_spec = pltpu.PrefetchScalarGridSpec(
        num_scalar_prefetch=1, grid=(ngroups, E, ns),
        in_specs=[pl.BlockSpec((ns * ts, D), lambda g, e, s, c: (g, 0)), tile(), tile(),
                  pl.BlockSpec((None, D, F2), lambda g, e, s, c: (e, 0, 0)),
                  pl.BlockSpec((None, 1, F2), lambda g, e, s, c: (e, 0, 0)),
                  pl.BlockSpec((None, F2 // 2, D), lambda g, e, s, c: (e, 0, 0)),
                  pl.BlockSpec((None, 1, D), lambda g, e, s, c: (e, 0, 0))],
        out_specs=pl.BlockSpec((ns * ts, D), lambda g, e, s, c: (g, 0)))
    nst = T // ts
    return pl.pallas_call(
        functools.partial(_moe_kernel, ns=ns, ts=ts, rb=rb), grid_spec=grid_spec,
        out_shape=jax.ShapeDtypeStruct((T, D), F32),
        compiler_params=_params(("parallel", "arbitrary", "arbitrary")), name="moe",
    )(counts, hn, gate.reshape(nst, E, 1, ts), rank.reshape(nst, E, 1, ts), w1, b1, w2, b2)


def _final_kernel(h_ref, y_ref, g_ref, o_ref):
    o_ref[...] = _rms(h_ref[...] + y_ref[...], g_ref[...])


def _final_norm(h, y, g, tm):
    R, D = h.shape
    row = pl.BlockSpec((tm, D), lambda i: (i, 0))
    return pl.pallas_call(
        _final_kernel, grid=(R // tm,), in_specs=[row, row, _full((1, D))], out_specs=row,
        out_shape=jax.ShapeDtypeStruct((R, D), F32),
        compiler_params=_params(("parallel",)), name="final_norm",
    )(h, y, g)


def _rope_tables(pos):
    half = RET_DIM // 2
    inv_freq = 1.0 / (ROPE_BASE ** jnp.linspace(0.0, 1.0, half, dtype=F32))
    ang = pos.astype(F32)[:, None] * inv_freq[None, :]
    cos, sin = jnp.cos(ang), jnp.sin(ang)
    return jnp.concatenate([cos, cos], axis=1), jnp.concatenate([-sin, sin], axis=1)


def _tile_of(n, pref):
    t = min(n, pref)
    assert n % t == 0, (n, t)
    return t


def _moe_layer(h, lw, ts, ns, rb):
    g_moe, w_router_t, b_router, w1, b1, w2, b2 = lw
    hn, gate, rank, cnt = _router(h, g_moe, w_router_t, b_router, ts)
    counts = cnt[:, :, 0].reshape(-1)
    return _moe(hn, gate, rank, counts, w1, b1, w2, b2, ts, ns, rb)


def kernel(x_prompt, x_sample, mem_prompt, cache_fox_k, cache_fox_v, cache_fox_logf, state_ret, cache_mem_k, cache_mem_v, page_table, g_mix, w_in, b_forget, w_up_fox, w_up_ret, w_up_mem, w_out, g_mem, w_mem_kv, g_moe, w_router, b_router, w_ff1, b_ff1, w_ff2, b_ff2, g_final):
    B, S, D = x_prompt.shape
    DB, T, _ = x_sample.shape
    depth = g_mix.shape[0]
    assert B == 1 and depth == 1
    NP = page_table.shape[1]
    page = cache_fox_k.shape[2]
    past_len = NP * page
    fw, rw, mw = FOX_HEADS * FOX_DIM, RET_HEADS * RET_DIM, MEM_HEADS * MEM_DIM
    R = DB * T
    l = 0

    wi = w_in[l]
    c0 = 3 * fw
    c1 = c0 + FOX_HEADS
    c2 = c1 + 4 * rw
    c3 = c2 + mw
    w_ffp = jnp.zeros((D, 128), F32).at[:, :FOX_HEADS].set(wi[:, c0:c1])
    wparts = tuple(t.astype(BF16) for t in (wi[:, :c0], w_ffp, wi[:, c1:c2], wi[:, c2:c3], wi[:, c3:]))
    bfp = jnp.zeros((1, 128), F32).at[0, :FOX_HEADS].set(b_forget[l])
    gm = g_mix[l].reshape(1, D)
    wuf = w_up_fox[l].astype(BF16).reshape(FOX_HEADS, FOX_DIM, D)
    wur, wum, wo = (t[l].astype(BF16) for t in (w_up_ret, w_up_mem, w_out))
    moe_w = (g_moe[l].reshape(1, D), jnp.swapaxes(w_router[l], 0, 1), b_router[l].reshape(-1, 1),
             w_ff1[l].astype(BF16), b_ff1[l][:, None, :], w_ff2[l].astype(BF16), b_ff2[l][:, None, :])
    gf = g_final.reshape(1, D)

    xp = x_prompt.reshape(S, D)
    tmp = _tile_of(S, 512)
    cos_p, sin_p = _rope_tables(jnp.arange(S))
    mk, mv = _memory_kv(mem_prompt[0], g_mem[l].reshape(1, D), w_mem_kv[l].astype(BF16))
    fq, fk, fv, lf, rq, rk, rv, rg, mq, gates = _input_projection(xp, gm, wparts, bfp, cos_p, sin_p, tmp)
    qa, ka, vh = _fox_prep(fq, fk, fv, lf, tmp)
    tk = _tile_of(S, 1024)
    fox_o = _fox_prompt(qa, ka, vh, _tile_of(tk, 512), tk)
    ret_o, rs_p = _ret_prompt(rq, rk, rv)
    mem_o = _mem_prompt(mq, mk, mv, tmp)
    hp = _mixer_out(xp, fox_o, ret_o, rg, mem_o, gates, wuf, wur, wum, wo, _tile_of(S, 256))
    ts = _tile_of(S, 1024)
    yp = _moe_layer(hp, moe_w, ts, _tile_of(S // ts, 2), 128)
    y_prompt = _final_norm(hp, yp, gf, tmp)

    xs = x_sample.reshape(R, D)
    tms = _tile_of(R, 512)
    cos_s, sin_s = _rope_tables(past_len + jnp.arange(T))
    cos_s, sin_s = jnp.tile(cos_s, (DB, 1)), jnp.tile(sin_s, (DB, 1))
    fq_s, fk_s, fv_s, lf_s, rq_s, rk_s, rv_s, rg_s, mq_s, gates_s = _input_projection(
        xs, gm, wparts, bfp, cos_s, sin_s, tms)
    fox_os = _fox_sample(fq_s, fk_s, fv_s, lf_s, page_table, cache_fox_k[l], cache_fox_v[l], cache_fox_logf[l],
                         _tile_of(NP, 8))
    ret_os, rs_s = _ret_sample(rq_s, rk_s, rv_s, state_ret[l])
    mem_os = _mem_sample(mq_s, cache_mem_k[l], cache_mem_v[l])
    hs = _mixer_out(xs, fox_os, ret_os, rg_s, mem_os, gates_s, wuf, wur, wum, wo, _tile_of(R, 256))
    tss = _tile_of(R, 1024)
    ys = _moe_layer(hs, moe_w, tss, 1, 128)
    y_sample = _final_norm(hs, ys, gf, tms)

    return (y_prompt.reshape(B, S, D), y_sample.reshape(DB, T, D),
            fk.reshape(1, B, S, FOX_HEADS, FOX_DIM), fv.reshape(1, B, S, FOX_HEADS, FOX_DIM),
            lf.reshape(1, B, S, FOX_HEADS), rs_p.reshape(1, B, RET_HEADS, RET_DIM, RET_DIM),
            mk.reshape(1, B, -1, MEM_HEADS, MEM_DIM), mv.reshape(1, B, -1, MEM_HEADS, MEM_DIM),
            fk_s.reshape(1, DB, T, FOX_HEADS, FOX_DIM), fv_s.reshape(1, DB, T, FOX_HEADS, FOX_DIM),
            lf_s.reshape(1, DB, T, FOX_HEADS), rs_s.reshape(1, DB, RET_HEADS, RET_DIM, RET_DIM))
```

```python
import functools
import math

import jax
import jax.numpy as jnp
import numpy as np
from jax import lax
from jax.experimental import pallas as pl
from jax.experimental.pallas import tpu as pltpu

F32 = jnp.float32
BF16 = jnp.bfloat16
I32 = jnp.int32

FOX_HEADS = 8
FOX_DIM = 64
RET_HEADS = 4
RET_DIM = 128
RET_CHUNK = 128
MEM_HEADS = 4
MEM_DIM = 128
N_BRANCH = 3
TOP_K = 4
ROPE_BASE = 10000.0
SWIGLU_LIMIT = 7.0
SWIGLU_ALPHA = 1.702
EPS = 1e-6

LOG2E = 1.4426950408889634
NEG = -1e30
VMEM_LIMIT = 48 * 1024 * 1024

HI = lax.Precision.HIGHEST
NT = (((1,), (1,)), ((), ()))
TN = (((0,), (0,)), ((), ()))


def _params(sem, vmem=VMEM_LIMIT):
    return pltpu.CompilerParams(dimension_semantics=sem, vmem_limit_bytes=vmem)


def _rms(x, g):
    return x * lax.rsqrt(jnp.mean(x * x, axis=-1, keepdims=True) + EPS) * g


def _log_sigmoid(x):
    return jnp.minimum(x, 0.0) - jnp.log(1.0 + jnp.exp(-jnp.abs(x)))


def _sigmoid(x):
    return 1.0 / (1.0 + jnp.exp(-x))


def _split3(x):
    hi = x.astype(BF16)
    r = x - hi.astype(F32)
    mid = r.astype(BF16)
    lo = (r - mid.astype(F32)).astype(BF16)
    return hi, mid, lo


def _full(shape):
    nd = len(shape)
    return pl.BlockSpec(shape, lambda *_: (0,) * nd)


def _proj_fox_kernel(x_ref, g_ref, w_ref, wf_ref, bf_ref, fq_ref, fk_ref, fv_ref, lf_ref):
    xn = _rms(x_ref[...], g_ref[...]).astype(BF16)
    y = jnp.dot(xn, w_ref[...], preferred_element_type=F32)
    w = fq_ref.shape[1]
    fq_ref[...] = y[:, :w]
    fk_ref[...] = y[:, w:2 * w]
    fv_ref[...] = y[:, 2 * w:]
    ff = jnp.dot(xn, wf_ref[...], preferred_element_type=F32)
    lf_ref[...] = _log_sigmoid(ff + bf_ref[...])[:, :lf_ref.shape[1]]


def _rotate(y, cos, sin):
    outs = []
    for h in range(RET_HEADS):
        yh = y[:, h * RET_DIM:(h + 1) * RET_DIM]
        outs.append(yh * cos + pltpu.roll(yh, RET_DIM // 2, axis=1) * sin)
    return jnp.concatenate(outs, axis=1)


def _proj_ret_kernel(x_ref, g_ref, w_ref, wm_ref, cos_ref, sin_ref, rq_ref, rk_ref, rv_ref, rg_ref, mq_ref):
    xn = _rms(x_ref[...], g_ref[...]).astype(BF16)
    y = jnp.dot(xn, w_ref[...], preferred_element_type=F32)
    w = rq_ref.shape[1]
    cos, sin = cos_ref[...], sin_ref[...]
    rq_ref[...] = _rotate(y[:, :w], cos, sin)
    rk_ref[...] = _rotate(y[:, w:2 * w], cos, sin) * (RET_DIM ** -0.5)
    rv_ref[...] = y[:, 2 * w:3 * w]
    rg_ref[...] = y[:, 3 * w:]
    mq_ref[...] = jnp.dot(xn, wm_ref[...], preferred_element_type=F32)


def _input_projection(x, g, wparts, b_forget_pad, cos, sin, tm):
    R, D = x.shape
    w_fox, w_ff, w_ret, w_mem, _ = wparts
    fw = FOX_HEADS * FOX_DIM
    rw = RET_HEADS * RET_DIM
    mw = MEM_HEADS * MEM_DIM
    row = lambda n: pl.BlockSpec((tm, n), lambda i: (i, 0))
    sds = lambda n: jax.ShapeDtypeStruct((R, n), F32)
    grid = (R // tm,)
    fq, fk, fv, lf = pl.pallas_call(
        _proj_fox_kernel, grid=grid,
        in_specs=[row(D), _full((1, D)), _full(w_fox.shape), _full(w_ff.shape), _full((1, 128))],
        out_specs=[row(fw), row(fw), row(fw), row(FOX_HEADS)],
        out_shape=[sds(fw), sds(fw), sds(fw), sds(FOX_HEADS)],
        compiler_params=_params(("parallel",)), name="proj_fox",
    )(x, g, w_fox, w_ff, b_forget_pad)
    rq, rk, rv, rg, mq = pl.pallas_call(
        _proj_ret_kernel, grid=grid,
        in_specs=[row(D), _full((1, D)), _full(w_ret.shape), _full(w_mem.shape), row(RET_DIM), row(RET_DIM)],
        out_specs=[row(rw), row(rw), row(rw), row(rw), row(mw)],
        out_shape=[sds(rw), sds(rw), sds(rw), sds(rw), sds(mw)],
        compiler_params=_params(("parallel",)), name="proj_ret",
    )(x, g, w_ret, w_mem, cos, sin)
    return fq, fk, fv, lf, rq, rk, rv, rg, mq


def _memkv_kernel(x_ref, g_ref, w_ref, mk_ref, mv_ref):
    xn = _rms(x_ref[...], g_ref[...]).astype(BF16)
    y = jnp.dot(xn, w_ref[...], preferred_element_type=F32)
    w = mk_ref.shape[1]
    mk_ref[...] = y[:, :w]
    mv_ref[...] = y[:, w:]


def _memory_kv(mem, g, w):
    M, D = mem.shape
    mw = MEM_HEADS * MEM_DIM
    return pl.pallas_call(
        _memkv_kernel, grid=(1,),
        in_specs=[_full((M, D)), _full((1, D)), _full(w.shape)],
        out_specs=[_full((M, mw)), _full((M, mw))],
        out_shape=[jax.ShapeDtypeStruct((M, mw), F32)] * 2,
        compiler_params=_params(("arbitrary",)), name="memory_kv",
    )(mem, g, w)


def _fox_prep_kernel(fq_ref, fk_ref, fv_ref, lf_ref, tri_ref, qa_ref, ka_ref, vt_ref, carry_ref):
    @pl.when(pl.program_id(0) == 0)
    def _():
        carry_ref[...] = jnp.zeros_like(carry_ref)

    tm = fq_ref.shape[0]
    c = jnp.dot(tri_ref[...], lf_ref[...], preferred_element_type=F32, precision=HI) + carry_ref[...]
    carry_ref[...] = c[tm - 1:tm, :]
    hi, mid, lo = (p.astype(F32) for p in _split3(c * LOG2E))
    lane = lax.broadcasted_iota(I32, (tm, 2 * FOX_DIM), 1)
    d = FOX_DIM
    fq, fk, fv = fq_ref[...], fk_ref[...], fv_ref[...]
    for h in range(FOX_HEADS):
        g = h // 2
        sl = slice(g * 2 * d, (g + 1) * 2 * d)
        qg, kg = fq[:, sl], fk[:, sl]
        if h % 2:
            qg, kg = (pltpu.roll(t, d, axis=1) for t in (qg, kg))
        ch, cm, cl = hi[:, h:h + 1], mid[:, h:h + 1], lo[:, h:h + 1]
        one = jnp.where((lane >= d + 3) & (lane < d + 6), 1.0, 0.0)
        qa = jnp.where(lane < d, qg * (d ** -0.5 * LOG2E),
                       jnp.where(lane == d, ch, jnp.where(lane == d + 1, cm, jnp.where(lane == d + 2, cl, one))))
        onek = jnp.where((lane >= d) & (lane < d + 3), 1.0, 0.0)
        ka = jnp.where(lane < d, kg,
                       jnp.where(lane == d + 3, -ch, jnp.where(lane == d + 4, -cm, jnp.where(lane == d + 5, -cl, onek))))
        qa_ref[h] = qa.astype(BF16)
        ka_ref[h] = ka.astype(BF16)
    vt_ref[...] = fv.T.astype(BF16)


def _fox_prep(fq, fk, fv, lf, tm):
    S, W = fq.shape
    tri = jnp.tril(jnp.ones((tm, tm), F32))
    row = lambda n: pl.BlockSpec((tm, n), lambda i: (i, 0))
    hm = lambda n: pl.BlockSpec((FOX_HEADS, tm, n), lambda i: (0, i, 0))
    return pl.pallas_call(
        _fox_prep_kernel, grid=(S // tm,),
        in_specs=[row(W), row(W), row(W), row(FOX_HEADS), _full((tm, tm))],
        out_specs=[hm(2 * FOX_DIM), hm(2 * FOX_DIM), pl.BlockSpec((W, tm), lambda i: (0, i))],
        out_shape=[jax.ShapeDtypeStruct((FOX_HEADS, S, 2 * FOX_DIM), BF16)] * 2
        + [jax.ShapeDtypeStruct((W, S), BF16)],
        scratch_shapes=[pltpu.VMEM((1, FOX_HEADS), F32)],
        compiler_params=_params(("arbitrary",)), name="fox_prep",
    )(fq, fk, fv, lf, tri)


def _fox_prompt_kernel(q_ref, k_ref, vt_ref, o_ref, *, tq, tk):
    i = pl.program_id(1)
    q = q_ref[...]
    npair = (i * tq) // (2 * tk)

    def pair(jp, carry, masked):
        m, l, acc = carry
        sts, k0s = [], []
        for u in range(2):
            k0 = pl.multiple_of((2 * jp + u) * tk, tk)
            st = lax.dot_general(k_ref[pl.ds(k0, tk), :], q, NT, preferred_element_type=F32)
            if masked:
                keyg = k0 + lax.broadcasted_iota(I32, (tk, tq), 0)
                qryg = i * tq + lax.broadcasted_iota(I32, (tk, tq), 1)
                st = jnp.where(keyg <= qryg, st, NEG)
            sts.append(st)
            k0s.append(k0)
        for st, k0 in zip(sts, k0s):
            m_new = jnp.maximum(m, jnp.max(st, axis=0, keepdims=True))
            alpha = jnp.exp2(m - m_new)
            p = jnp.exp2(st - m_new)
            l = alpha * l + jnp.sum(p, axis=0, keepdims=True)
            acc = alpha * acc + jnp.dot(vt_ref[:, pl.ds(k0, tk)], p.astype(BF16), preferred_element_type=F32)
            m = m_new
        return m, l, acc

    init = (jnp.full((1, tq), NEG, F32), jnp.zeros((1, tq), F32), jnp.zeros((FOX_DIM, tq), F32))
    carry = lax.fori_loop(0, npair, lambda jp, c: pair(jp, c, False), init)
    m, l, acc = pair(npair, carry, True)
    o_ref[...] = (acc / l).astype(o_ref.dtype)


def _fox_prompt(qa, ka, vt, tq, tk):
    H, S, A = qa.shape
    return pl.pallas_call(
        functools.partial(_fox_prompt_kernel, tq=tq, tk=tk), grid=(H, S // tq),
        in_specs=[pl.BlockSpec((None, tq, A), lambda h, i: (h, i, 0)),
                  pl.BlockSpec((None, S, A), lambda h, i: (h, 0, 0)),
                  pl.BlockSpec((FOX_DIM, S), lambda h, i: (h, 0))],
        out_specs=pl.BlockSpec((FOX_DIM, tq), lambda h, i: (h, i)),
        out_shape=jax.ShapeDtypeStruct((H * FOX_DIM, S), BF16),
        compiler_params=_params(("parallel", "arbitrary")), name="fox_prompt",
    )(qa, ka, vt)


def _fox_sample_kernel(pt_ref, q_ref, kn_ref, vn_ref, lfn_ref, lfnt_ref, ms_ref, *rest, pp):
    k_refs, v_refs, l_refs = rest[:pp], rest[pp:2 * pp], rest[2 * pp:3 * pp]
    o_ref = rest[3 * pp]
    m_ref, l_ref, acc_ref, carry_ref, qbd_ref, rowc_ref = rest[3 * pp + 1:]
    j = pl.program_id(1)
    H, d, T = FOX_HEADS, FOX_DIM, q_ref.shape[0]
    HT, W = H * T, H * d
    page = ms_ref.shape[0]

    @pl.when(j == 0)
    def _():
        qsc = q_ref[...] * (d ** -0.5 * LOG2E)
        lfn, lfnt = lfn_ref[...], lfnt_ref[...]
        ti = lax.broadcasted_iota(I32, (T, T), 0)
        tj = lax.broadcasted_iota(I32, (T, T), 1)
        low = jnp.where(tj <= ti, 1.0, 0.0)
        up = jnp.where(ti <= tj, 1.0, 0.0)
        cq = jnp.zeros((T, H), F32)
        cqt = jnp.zeros((H, T), F32)
        for t in range(T):
            cq = cq + low[:, t:t + 1] * lfn[t:t + 1, :]
            cqt = cqt + lfnt[:, t:t + 1] * up[t:t + 1, :]
        bias = jnp.concatenate([cq[:, h:h + 1] - cqt[h:h + 1, :] for h in range(H)], axis=0) * LOG2E
        rowc_ref[...] = jnp.concatenate([cq[:, h:h + 1] for h in range(H)], axis=0) * LOG2E
        r_i = lax.broadcasted_iota(I32, (HT, W), 0)
        c_i = lax.broadcasted_iota(I32, (HT, W), 1)
        qbd = jnp.where(c_i // d == r_i // T, jnp.concatenate([qsc] * H, axis=0), 0.0).astype(BF16)
        qbd_ref[...] = qbd
        s = lax.dot_general(qbd, kn_ref[...].astype(BF16), NT, preferred_element_type=F32) + bias
        rr = lax.broadcasted_iota(I32, (HT, T), 0)
        cc = lax.broadcasted_iota(I32, (HT, T), 1)
        s = jnp.where(cc <= rr % T, s, NEG)
        m0 = jnp.max(s, axis=1, keepdims=True)
        p = jnp.exp2(s - m0)
        m_ref[...] = m0
        l_ref[...] = jnp.sum(p, axis=1, keepdims=True)
        acc_ref[...] = jnp.dot(p.astype(BF16), vn_ref[...].astype(BF16), preferred_element_type=F32)
        carry_ref[...] = jnp.zeros_like(carry_ref)

    qbd = qbd_ref[...]
    rowc = rowc_ref[...]
    ms = ms_ref[...]
    lf = jnp.concatenate([l_refs[r][...] for r in range(pp)], axis=0)
    pieces = jnp.concatenate([pc.astype(F32) for pc in _split3(lf)], axis=0).astype(BF16)
    rr3 = jnp.dot(pieces, ms, preferred_element_type=F32)
    n = pp * H
    rr = rr3[:n] + rr3[n:2 * n] + rr3[2 * n:]
    carry = carry_ref[...]
    biases = []
    for r in range(pp):
        rs = (rr[r * H:(r + 1) * H, :page] + carry[:, 0:1]) * LOG2E
        carry = carry + rr[r * H:(r + 1) * H, page:]
        biases.append(jnp.concatenate([jnp.broadcast_to(rs[h:h + 1, :], (T, page)) for h in range(H)], axis=0))
    carry_ref[...] = carry
    bias = jnp.concatenate(biases, axis=1) + rowc
    kb = jnp.concatenate([k_refs[r][...] for r in range(pp)], axis=1).astype(BF16)
    vb = jnp.concatenate([v_refs[r][...] for r in range(pp)], axis=1).astype(BF16)
    s = jnp.dot(qbd, kb, preferred_element_type=F32) + bias
    m = m_ref[...]
    m_new = jnp.maximum(m, jnp.max(s, axis=1, keepdims=True))
    alpha = jnp.exp2(m - m_new)
    p = jnp.exp2(s - m_new)
    l_ref[...] = alpha * l_ref[...] + jnp.sum(p, axis=1, keepdims=True)
    acc_ref[...] = alpha * acc_ref[...] + lax.dot_general(p.astype(BF16), vb, NT, preferred_element_type=F32)
    m_ref[...] = m_new

    @pl.when(j == pl.num_programs(1) - 1)
    def _():
        acc = acc_ref[...]
        o = jnp.concatenate([acc[h * T:(h + 1) * T, h * d:(h + 1) * d] for h in range(H)], axis=0)
        o_ref[...] = (o / l_ref[...]).reshape(H, T, d)


def _fox_sample(q, kn, vn, lfn, page_table, cache_k, cache_v, cache_lf, pp):
    R, W = q.shape
    DB, NP = page_table.shape
    T = R // DB
    n_pool, page, H, d = cache_k.shape
    ck = jnp.transpose(cache_k, (0, 2, 3, 1)).reshape(n_pool, H * d, page)
    cv = jnp.transpose(cache_v, (0, 2, 3, 1)).reshape(n_pool, H * d, page)
    cl = jnp.transpose(cache_lf, (0, 2, 1))
    lfnt = jnp.swapaxes(lfn.reshape(DB, T, H), 1, 2)
    tt = np.arange(page)
    ms = np.concatenate([tt[:, None] > tt[None, :], np.ones((page, 128), bool)], axis=1)
    ms = jnp.asarray(ms, BF16)
    pt = page_table.reshape(-1).astype(I32)

    def page_spec(shape, r):
        return pl.BlockSpec((None,) + shape, lambda b, j, pt: (pt[b * NP + NP - 1 - (j * pp + r)], 0, 0))

    tok = lambda n: pl.BlockSpec((T, n), lambda b, j, pt: (b, 0))
    in_specs = [tok(W), tok(W), tok(W), tok(H), pl.BlockSpec((None, H, T), lambda b, j, pt: (b, 0, 0)),
                pl.BlockSpec(ms.shape, lambda b, j, pt: (0, 0))]
    in_specs += [page_spec((H * d, page), r) for r in range(pp)]
    in_specs += [page_spec((H * d, page), r) for r in range(pp)]
    in_specs += [page_spec((H, page), r) for r in range(pp)]
    grid_spec = pltpu.PrefetchScalarGridSpec(
        num_scalar_prefetch=1, grid=(DB, NP // pp), in_specs=in_specs,
        out_specs=pl.BlockSpec((H, T, d), lambda b, j, pt: (0, b, 0)),
        scratch_shapes=[pltpu.VMEM((H * T, 1), F32), pltpu.VMEM((H * T, 1), F32), pltpu.VMEM((H * T, W), F32),
                        pltpu.VMEM((H, 128), F32), pltpu.VMEM((H * T, W), BF16), pltpu.VMEM((H * T, 1), F32)])
    return pl.pallas_call(
        functools.partial(_fox_sample_kernel, pp=pp), grid_spec=grid_spec,
        out_shape=jax.ShapeDtypeStruct((H, R, d), F32),
        compiler_params=_params(("parallel", "arbitrary")), name="fox_sample",
    )(pt, q, kn, vn, lfn, lfnt, ms, *([ck] * pp), *([cv] * pp), *([cl] * pp))


def _ret_consts(C):
    lg = jnp.log(1.0 - jnp.exp2(-5.0 - jnp.arange(RET_HEADS, dtype=F32)))
    idx = jnp.arange(C, dtype=F32)
    rel = idx[:, None] - idx[None, :]
    intra = jnp.where(rel >= 0, jnp.exp(jnp.maximum(rel, 0.0)[None] * lg[:, None, None]), 0.0)
    qd = jnp.exp((idx + 1.0)[None, :] * lg[:, None])
    kd = jnp.exp((C - 1.0 - idx)[None, :] * lg[:, None])
    cd = jnp.exp(C * lg)
    rep = lambda t: jnp.broadcast_to(t[:, :, None], t.shape + (RET_DIM,))
    return intra, rep(qd), rep(kd), jnp.broadcast_to(cd[:, None, None], (RET_HEADS, 1, RET_DIM))


def _ret_heads(q, k, v, state_of, intra_ref, qd_ref, kd_ref, cd_ref):
    outs, states = [], []
    for h in range(RET_HEADS):
        sl = slice(h * RET_DIM, (h + 1) * RET_DIM)
        qh, kh, vh = q[:, sl].astype(BF16), k[:, sl], v[:, sl].astype(BF16)
        st = state_of(h)
        a = lax.dot_general(qh, kh.astype(BF16), NT, preferred_element_type=F32) * intra_ref[h]
        o = (jnp.dot(a.astype(BF16), vh, preferred_element_type=F32)
             + jnp.dot(qh, st.astype(BF16), preferred_element_type=F32) * qd_ref[h])
        kdec = (kh * kd_ref[h]).astype(BF16)
        states.append(cd_ref[h] * st + lax.dot_general(kdec, vh, TN, preferred_element_type=F32))
        outs.append(o)
    return jnp.concatenate(outs, axis=1), states


def _ret_prompt_kernel(q_ref, k_ref, v_ref, intra_ref, qd_ref, kd_ref, cd_ref, o_ref, so_ref, st_ref):
    @pl.when(pl.program_id(0) == 0)
    def _():
        st_ref[...] = jnp.zeros_like(st_ref)

    o, states = _ret_heads(q_ref[...], k_ref[...], v_ref[...], lambda h: st_ref[h],
                           intra_ref, qd_ref, kd_ref, cd_ref)
    o_ref[...] = o
    for h in range(RET_HEADS):
        st_ref[h] = states[h]

    @pl.when(pl.program_id(0) == pl.num_programs(0) - 1)
    def _():
        so_ref[...] = st_ref[...]


def _ret_prompt(rq, rk, rv):
    S, W = rq.shape
    C = min(RET_CHUNK, S)
    consts = _ret_consts(C)
    row = pl.BlockSpec((C, W), lambda i: (i, 0))
    st_shape = (RET_HEADS, RET_DIM, RET_DIM)
    return pl.pallas_call(
        _ret_prompt_kernel, grid=(S // C,),
        in_specs=[row, row, row] + [_full(c.shape) for c in consts],
        out_specs=[row, _full(st_shape)],
        out_shape=[jax.ShapeDtypeStruct((S, W), F32), jax.ShapeDtypeStruct(st_shape, F32)],
        scratch_shapes=[pltpu.VMEM(st_shape, F32)],
        compiler_params=_params(("arbitrary",)), name="ret_prompt",
    )(rq, rk, rv, *consts)


def _ret_sample_kernel(q_ref, k_ref, v_ref, s0_ref, intra_ref, qd_ref, kd_ref, cd_ref, o_ref, so_ref):
    o, states = _ret_heads(q_ref[...], k_ref[...], v_ref[...], lambda h: s0_ref[h],
                           intra_ref, qd_ref, kd_ref, cd_ref)
    o_ref[...] = o
    for h in range(RET_HEADS):
        so_ref[h] = states[h]


def _ret_sample(rq, rk, rv, state0):
    R, W = rq.shape
    DB = state0.shape[0]
    T = R // DB
    consts = _ret_consts(min(RET_CHUNK, T))
    row = pl.BlockSpec((T, W), lambda b: (b, 0))
    st = pl.BlockSpec((None, RET_HEADS, RET_DIM, RET_DIM), lambda b: (b, 0, 0, 0))
    return pl.pallas_call(
        _ret_sample_kernel, grid=(DB,),
        in_specs=[row, row, row, st] + [_full(c.shape) for c in consts],
        out_specs=[row, st],
        out_shape=[jax.ShapeDtypeStruct((R, W), F32), jax.ShapeDtypeStruct(state0.shape, F32)],
        compiler_params=_params(("parallel",)), name="ret_sample",
    )(rq, rk, rv, state0, *consts)


def _softmax_rows(s):
    e = jnp.exp(s - jnp.max(s, axis=1, keepdims=True))
    return e / jnp.sum(e, axis=1, keepdims=True)


def _mem_prompt_kernel(q_ref, k_ref, v_ref, o_ref):
    q, k, v = q_ref[...], k_ref[...], v_ref[...]
    outs = []
    for h in range(MEM_HEADS):
        sl = slice(h * MEM_DIM, (h + 1) * MEM_DIM)
        s = lax.dot_general(q[:, sl].astype(BF16), k[:, sl].astype(BF16), NT,
                            preferred_element_type=F32) * (MEM_DIM ** -0.5)
        p = _softmax_rows(s).astype(BF16)
        outs.append(jnp.dot(p, v[:, sl].astype(BF16), preferred_element_type=F32))
    o_ref[...] = jnp.concatenate(outs, axis=1).astype(o_ref.dtype)


def _mem_prompt(mq, mk, mv, tm):
    S, W = mq.shape
    row = pl.BlockSpec((tm, W), lambda i: (i, 0))
    return pl.pallas_call(
        _mem_prompt_kernel, grid=(S // tm,),
        in_specs=[row, _full(mk.shape), _full(mv.shape)],
        out_specs=row, out_shape=jax.ShapeDtypeStruct((S, W), BF16),
        compiler_params=_params(("parallel",)), name="mem_prompt",
    )(mq, mk, mv)


def _mem_sample_kernel(q_ref, k_ref, v_ref, mask_ref, o_ref):
    q = q_ref[...]
    T = q.shape[0]
    qm = jnp.concatenate([q[:, h * MEM_DIM:(h + 1) * MEM_DIM] for h in range(MEM_HEADS)], axis=0).astype(BF16)
    s = lax.dot_general(qm, k_ref[...].astype(BF16), NT, preferred_element_type=F32) * (MEM_DIM ** -0.5)
    p = _softmax_rows(s + mask_ref[...]).astype(BF16)
    o = jnp.dot(p, v_ref[...].astype(BF16), preferred_element_type=F32)
    o_ref[...] = jnp.concatenate([o[h * T:(h + 1) * T, :] for h in range(MEM_HEADS)], axis=1)


def _mem_sample(mq, cache_k, cache_v):
    R, W = mq.shape
    DB, M, H, d = cache_k.shape
    T = R // DB
    kf = cache_k.reshape(DB, M * H, d)
    vf = cache_v.reshape(DB, M * H, d)
    mask = np.where((np.arange(M * H) % H)[None, :] == (np.arange(H * T) // T)[:, None], 0.0, NEG)
    mask = jnp.asarray(mask, F32)
    row = pl.BlockSpec((T, W), lambda b: (b, 0))
    kv = pl.BlockSpec((None, M * H, d), lambda b: (b, 0, 0))
    return pl.pallas_call(
        _mem_sample_kernel, grid=(DB,),
        in_specs=[row, kv, kv, _full(mask.shape)],
        out_specs=row, out_shape=jax.ShapeDtypeStruct((R, W), F32),
        compiler_params=_params(("parallel",)), name="mem_sample",
    )(mq, kf, vf, mask)


def _mixer_out_kernel(x_ref, g_ref, fox_ref, ret_ref, rg_ref, mem_ref, wg_ref, wf_ref, wr_ref, wm_ref, wo_ref, o_ref,
                      *, fox_transposed):
    D = x_ref.shape[1]
    x = x_ref[...]
    gates = jnp.dot(_rms(x, g_ref[...]).astype(BF16), wg_ref[...], preferred_element_type=F32)
    if fox_transposed:
        u_fox = lax.dot_general(fox_ref[...], wf_ref[...].reshape(-1, D), TN, preferred_element_type=F32)
    else:
        u_fox = jnp.dot(fox_ref[0].astype(BF16), wf_ref[0], preferred_element_type=F32)
        for h in range(1, FOX_HEADS):
            u_fox += jnp.dot(fox_ref[h].astype(BF16), wf_ref[h], preferred_element_type=F32)
    ret, rg = ret_ref[...], rg_ref[...]
    ys = []
    for h in range(RET_HEADS):
        sl = slice(h * RET_DIM, (h + 1) * RET_DIM)
        rf = ret[:, sl]
        rn = rf * lax.rsqrt(jnp.mean(rf * rf, axis=1, keepdims=True) + EPS)
        g = rg[:, sl]
        ys.append((rn * (g * _sigmoid(g))).astype(BF16))
    u_ret = jnp.dot(jnp.concatenate(ys, axis=1), wr_ref[...], preferred_element_type=F32)
    u_mem = jnp.dot(mem_ref[...].astype(BF16), wm_ref[...], preferred_element_type=F32)
    merged = (_sigmoid(gates[:, :D]) * u_fox + _sigmoid(gates[:, D:2 * D]) * u_ret
              + _sigmoid(gates[:, 2 * D:]) * u_mem)
    o_ref[...] = x + jnp.dot(merged.astype(BF16), wo_ref[...], preferred_element_type=F32)


def _mixer_out(x, g, fox_o, ret_o, rg, mem_o, w_gate, w_up_fox, w_up_ret, w_up_mem, w_out, tm):
    R, D = x.shape
    row = lambda n: pl.BlockSpec((tm, n), lambda i: (i, 0))
    fox_transposed = fox_o.ndim == 2
    if fox_transposed:
        fox_spec = pl.BlockSpec((fox_o.shape[0], tm), lambda i: (0, i))
    else:
        fox_spec = pl.BlockSpec((FOX_HEADS, tm, FOX_DIM), lambda i: (0, i, 0))
    return pl.pallas_call(
        functools.partial(_mixer_out_kernel, fox_transposed=fox_transposed), grid=(R // tm,),
        in_specs=[row(D), _full((1, D)), fox_spec,
                  row(ret_o.shape[1]), row(rg.shape[1]), row(mem_o.shape[1]), _full(w_gate.shape),
                  _full(w_up_fox.shape), _full(w_up_ret.shape), _full(w_up_mem.shape), _full(w_out.shape)],
        out_specs=row(D), out_shape=jax.ShapeDtypeStruct((R, D), F32),
        compiler_params=_params(("parallel",)), name="mixer_out",
    )(x, g, fox_o, ret_o, rg, mem_o, w_gate, w_up_fox, w_up_ret, w_up_mem, w_out)


def _router_kernel(h_ref, g_ref, wr_ref, br_ref, tri_ref, hn_ref, gate_ref, rank_ref, cnt_ref):
    hn = _rms(h_ref[...], g_ref[...])
    hn_ref[...] = hn.astype(BF16)
    logits = lax.dot_general(wr_ref[...], hn, NT, preferred_element_type=F32, precision=HI) + br_ref[...]
    E, TS = logits.shape
    e_iota = lax.broadcasted_iota(I32, (E, TS), 0)
    work = logits
    vals, hots = [], []
    for _ in range(TOP_K):
        mx = jnp.max(work, axis=0, keepdims=True)
        idx = jnp.min(jnp.where(work == mx, e_iota, E), axis=0, keepdims=True)
        hot = e_iota == idx
        vals.append(mx)
        hots.append(hot)
        work = jnp.where(hot, -jnp.inf, work)
    es = [jnp.exp(v - vals[0]) for v in vals]
    den = es[0]
    for e in es[1:]:
        den = den + e
    gate = jnp.zeros((E, TS), F32)
    sel = jnp.zeros((E, TS), F32)
    for hot, e in zip(hots, es):
        gate = jnp.where(hot, e / den, gate)
        sel = jnp.where(hot, 1.0, sel)
    cum = jnp.dot(sel.astype(BF16), tri_ref[...], preferred_element_type=F32)
    gate_ref[...] = gate
    rank_ref[...] = jnp.where(sel > 0.0, cum - 1.0, -1.0).astype(I32)
    cnt_ref[...] = jnp.broadcast_to(cum[:, TS - 1:TS], cnt_ref.shape).astype(I32)


def _router(h, g, w_router_t, b_router, ts):
    T, D = h.shape
    E = w_router_t.shape[0]
    ns = T // ts
    tri = jnp.triu(jnp.ones((ts, ts), BF16))
    tile = lambda dt: jax.ShapeDtypeStruct((ns, E, ts), dt)
    tspec = pl.BlockSpec((None, E, ts), lambda i: (i, 0, 0))
    return pl.pallas_call(
        _router_kernel, grid=(ns,),
        in_specs=[pl.BlockSpec((ts, D), lambda i: (i, 0)), _full((1, D)), _full((E, D)), _full((E, 1)),
                  _full((ts, ts))],
        out_specs=[pl.BlockSpec((ts, D), lambda i: (i, 0)), tspec, tspec,
                   pl.BlockSpec((None, E, 128), lambda i: (i, 0, 0))],
        out_shape=[jax.ShapeDtypeStruct((T, D), BF16), tile(F32), tile(I32),
                   jax.ShapeDtypeStruct((ns, E, 128), I32)],
        compiler_params=_params(("parallel",)), name="router",
    )(h, g, w_router_t, b_router, tri)


def _moe_kernel(cnt_ref, x_ref, gate_ref, rank_ref, w1_ref, b1_ref, w2_ref, b2_ref, o_ref, yw_ref, *, ns, ts, rb, cb):
    g, e, si = pl.program_id(0), pl.program_id(1), pl.program_id(2)

    @pl.when((e == 0) & (si == 0))
    def _():
        o_ref[...] = jnp.zeros_like(o_ref)
        yw_ref[...] = jnp.zeros_like(yw_ref)

    E = pl.num_programs(1)
    cnt = cnt_ref[(g * ns + si) * E + e]
    F = w2_ref.shape[0]
    t0 = pl.multiple_of(si * ts, ts)

    def block(bi, carry):
        rank = rank_ref[...]
        r0 = pl.multiple_of(bi * rb, rb)
        hit = r0 + lax.broadcasted_iota(I32, (rb, ts), 0) == rank
        onehot = jnp.where(hit, 1.0, 0.0).astype(BF16)
        xb = jnp.dot(onehot, x_ref[pl.ds(t0, ts), :], preferred_element_type=F32).astype(BF16)
        hdn = jnp.dot(xb, w1_ref[...], preferred_element_type=F32) + b1_ref[...]
        glu = jnp.minimum(hdn[:, :F], SWIGLU_LIMIT)
        lin = jnp.clip(hdn[:, F:], -SWIGLU_LIMIT, SWIGLU_LIMIT)
        act = (glu * _sigmoid(SWIGLU_ALPHA * glu) * (lin + 1.0)).astype(BF16)
        y = jnp.dot(act, w2_ref[...], preferred_element_type=F32) + b2_ref[...]
        wrow = jnp.sum(jnp.where(hit, gate_ref[...], 0.0), axis=1, keepdims=True)
        yw_ref[pl.ds(r0, rb), :] = (y * wrow).astype(BF16)
        return carry

    lax.fori_loop(0, (cnt + rb - 1) // rb, block, 0)

    def chunk(ci, carry):
        r0 = pl.multiple_of(ci * cb, cb)
        hit = r0 + lax.broadcasted_iota(I32, (cb, ts), 0) == rank_ref[...]
        onehot = jnp.where(hit, 1.0, 0.0).astype(BF16)
        o_ref[pl.ds(t0, ts), :] += lax.dot_general(onehot, yw_ref[pl.ds(r0, cb), :], TN,
                                                   preferred_element_type=F32)
        return carry

    lax.fori_loop(0, (cnt + cb - 1) // cb, chunk, 0)


def _moe(hn, gate, rank, counts, w1, b1, w2, b2, ts, ns, rb, cb):
    T, D = hn.shape
    E, _, F2 = w1.shape
    ngroups = T // (ts * ns)
    tile = lambda: pl.BlockSpec((None, None, 1, ts), lambda g, e, s, c: (g * ns + s, e, 0, 0))
    grid_spec = pltpu.PrefetchScalarGridSpec(
        num_scalar_prefetch=1, grid=(ngroups, E, ns),
        in_specs=[pl.BlockSpec((ns * ts, D), lambda g, e, s, c: (g, 0)), tile(), tile(),
                  pl.BlockSpec((None, D, F2), lambda g, e, s, c: (e, 0, 0)),
                  pl.BlockSpec((None, 1, F2), lambda g, e, s, c: (e, 0, 0)),
                  pl.BlockSpec((None, F2 // 2, D), lambda g, e, s, c: (e, 0, 0)),
                  pl.BlockSpec((None, 1, D), lambda g, e, s, c: (e, 0, 0))],
        out_specs=pl.BlockSpec((ns * ts, D), lambda g, e, s, c: (g, 0)),
        scratch_shapes=[pltpu.VMEM((pl.cdiv(ts, cb) * cb, D), BF16)])
    nst = T // ts
    return pl.pallas_call(
        functools.partial(_moe_kernel, ns=ns, ts=ts, rb=rb, cb=cb), grid_spec=grid_spec,
        out_shape=jax.ShapeDtypeStruct((T, D), F32),
        compiler_params=_params(("parallel", "arbitrary", "arbitrary")), name="moe",
    )(counts, hn, gate.reshape(nst, E, 1, ts), rank.reshape(nst, E, 1, ts), w1, b1, w2, b2)


def _final_kernel(h_ref, y_ref, g_ref, o_ref):
    o_ref[...] = _rms(h_ref[...] + y_ref[...], g_ref[...])


def _final_norm(h, y, g, tm):
    R, D = h.shape
    row = pl.BlockSpec((tm, D), lambda i: (i, 0))
    return pl.pallas_call(
        _final_kernel, grid=(R // tm,), in_specs=[row, row, _full((1, D))], out_specs=row,
        out_shape=jax.ShapeDtypeStruct((R, D), F32),
        compiler_params=_params(("parallel",)), name="final_norm",
    )(h, y, g)


def _rope_tables(pos):
    half = RET_DIM // 2
    inv_freq = 1.0 / (ROPE_BASE ** jnp.linspace(0.0, 1.0, half, dtype=F32))
    ang = pos.astype(F32)[:, None] * inv_freq[None, :]
    cos, sin = jnp.cos(ang), jnp.sin(ang)
    return jnp.concatenate([cos, cos], axis=1), jnp.concatenate([-sin, sin], axis=1)


def _tile_of(n, pref):
    t = min(n, pref)
    assert n % t == 0, (n, t)
    return t


MOE_ROW_BLOCK = 128
MOE_SCATTER_BLOCK = 256


def _moe_layer(h, lw, ts, ns):
    g_moe, w_router_t, b_router, w1, b1, w2, b2 = lw
    hn, gate, rank, cnt = _router(h, g_moe, w_router_t, b_router, ts)
    counts = cnt[:, :, 0].reshape(-1)
    return _moe(hn, gate, rank, counts, w1, b1, w2, b2, ts, ns, MOE_ROW_BLOCK, MOE_SCATTER_BLOCK)


def kernel(x_prompt, x_sample, mem_prompt, cache_fox_k, cache_fox_v, cache_fox_logf, state_ret, cache_mem_k, cache_mem_v, page_table, g_mix, w_in, b_forget, w_up_fox, w_up_ret, w_up_mem, w_out, g_mem, w_mem_kv, g_moe, w_router, b_router, w_ff1, b_ff1, w_ff2, b_ff2, g_final):
    B, S, D = x_prompt.shape
    DB, T, _ = x_sample.shape
    depth = g_mix.shape[0]
    assert B == 1 and depth == 1
    NP = page_table.shape[1]
    page = cache_fox_k.shape[2]
    past_len = NP * page
    fw, rw, mw = FOX_HEADS * FOX_DIM, RET_HEADS * RET_DIM, MEM_HEADS * MEM_DIM
    R = DB * T
    l = 0

    wi = w_in[l]
    c0 = 3 * fw
    c1 = c0 + FOX_HEADS
    c2 = c1 + 4 * rw
    c3 = c2 + mw
    w_ffp = jnp.zeros((D, 128), F32).at[:, :FOX_HEADS].set(wi[:, c0:c1])
    wparts = tuple(t.astype(BF16) for t in (wi[:, :c0], w_ffp, wi[:, c1:c2], wi[:, c2:c3], wi[:, c3:]))
    bfp = jnp.zeros((1, 128), F32).at[0, :FOX_HEADS].set(b_forget[l])
    gm = g_mix[l].reshape(1, D)
    wuf = w_up_fox[l].astype(BF16).reshape(FOX_HEADS, FOX_DIM, D)
    wur, wum, wo = (t[l].astype(BF16) for t in (w_up_ret, w_up_mem, w_out))
    moe_w = (g_moe[l].reshape(1, D), jnp.swapaxes(w_router[l], 0, 1), b_router[l].reshape(-1, 1),
             w_ff1[l].astype(BF16), b_ff1[l][:, None, :], w_ff2[l].astype(BF16), b_ff2[l][:, None, :])
    gf = g_final.reshape(1, D)

    xp = x_prompt.reshape(S, D)
    tmp = _tile_of(S, 512)
    cos_p, sin_p = _rope_tables(jnp.arange(S))
    mk, mv = _memory_kv(mem_prompt[0], g_mem[l].reshape(1, D), w_mem_kv[l].astype(BF16))
    fq, fk, fv, lf, rq, rk, rv, rg, mq = _input_projection(xp, gm, wparts, bfp, cos_p, sin_p, tmp)
    qa, ka, vh = _fox_prep(fq, fk, fv, lf, tmp)
    tk = _tile_of(S // 2, 512)
    fox_o = _fox_prompt(qa, ka, vh, tk, tk)
    ret_o, rs_p = _ret_prompt(rq, rk, rv)
    mem_o = _mem_prompt(mq, mk, mv, tmp)
    hp = _mixer_out(xp, gm, fox_o, ret_o, rg, mem_o, wparts[4], wuf, wur, wum, wo, _tile_of(S, 256))
    ts = _tile_of(S, 1024)
    yp = _moe_layer(hp, moe_w, ts, _tile_of(S // ts, 2))
    y_prompt = _final_norm(hp, yp, gf, tmp)

    xs = x_sample.reshape(R, D)
    tms = _tile_of(R, 512)
    cos_s, sin_s = _rope_tables(past_len + jnp.arange(T))
    cos_s, sin_s = jnp.tile(cos_s, (DB, 1)), jnp.tile(sin_s, (DB, 1))
    fq_s, fk_s, fv_s, lf_s, rq_s, rk_s, rv_s, rg_s, mq_s = _input_projection(
        xs, gm, wparts, bfp, cos_s, sin_s, tms)
    fox_os = _fox_sample(fq_s, fk_s, fv_s, lf_s, page_table, cache_fox_k[l], cache_fox_v[l], cache_fox_logf[l],
                         _tile_of(NP, 16))
    ret_os, rs_s = _ret_sample(rq_s, rk_s, rv_s, state_ret[l])
    mem_os = _mem_sample(mq_s, cache_mem_k[l], cache_mem_v[l])
    hs = _mixer_out(xs, gm, fox_os, ret_os, rg_s, mem_os, wparts[4], wuf, wur, wum, wo, _tile_of(R, 256))
    tss = _tile_of(R, 1024)
    ys = _moe_layer(hs, moe_w, tss, 1)
    y_sample = _final_norm(hs, ys, gf, tms)

    return (y_prompt.reshape(B, S, D), y_sample.reshape(DB, T, D),
            fk.reshape(1, B, S, FOX_HEADS, FOX_DIM), fv.reshape(1, B, S, FOX_HEADS, FOX_DIM),
            lf.reshape(1, B, S, FOX_HEADS), rs_p.reshape(1, B, RET_HEADS, RET_DIM, RET_DIM),
            mk.reshape(1, B, -1, MEM_HEADS, MEM_DIM), mv.reshape(1, B, -1, MEM_HEADS, MEM_DIM),
            fk_s.reshape(1, DB, T, FOX_HEADS, FOX_DIM), fv_s.reshape(1, DB, T, FOX_HEADS, FOX_DIM),
            lf_s.reshape(1, DB, T, FOX_HEADS), rs_s.reshape(1, DB, RET_HEADS, RET_DIM, RET_DIM))
```

```python
import functools
import math

import jax
import jax.numpy as jnp
import numpy as np
from jax import lax
from jax.experimental import pallas as pl
from jax.experimental.pallas import tpu as pltpu

F32 = jnp.float32
BF16 = jnp.bfloat16
I32 = jnp.int32

FOX_HEADS = 8
FOX_DIM = 64
RET_HEADS = 4
RET_DIM = 128
RET_CHUNK = 128
MEM_HEADS = 4
MEM_DIM = 128
N_BRANCH = 3
TOP_K = 4
ROPE_BASE = 10000.0
SWIGLU_LIMIT = 7.0
SWIGLU_ALPHA = 1.702
EPS = 1e-6

LOG2E = 1.4426950408889634
NEG = -1e30
VMEM_LIMIT = 48 * 1024 * 1024

HI = lax.Precision.HIGHEST
NT = (((1,), (1,)), ((), ()))
TN = (((0,), (0,)), ((), ()))


def _params(sem, vmem=VMEM_LIMIT):
    return pltpu.CompilerParams(dimension_semantics=sem, vmem_limit_bytes=vmem)


def _rms(x, g):
    return x * lax.rsqrt(jnp.mean(x * x, axis=-1, keepdims=True) + EPS) * g


def _log_sigmoid(x):
    return jnp.minimum(x, 0.0) - jnp.log(1.0 + jnp.exp(-jnp.abs(x)))


def _sigmoid(x):
    return 1.0 / (1.0 + jnp.exp(-x))


def _split3(x):
    hi = x.astype(BF16)
    r = x - hi.astype(F32)
    mid = r.astype(BF16)
    lo = (r - mid.astype(F32)).astype(BF16)
    return hi, mid, lo


def _full(shape):
    nd = len(shape)
    return pl.BlockSpec(shape, lambda *_: (0,) * nd)


def _proj_fox_kernel(x_ref, g_ref, w_ref, wf_ref, bf_ref, fq_ref, fk_ref, fv_ref, lf_ref):
    xn = _rms(x_ref[...], g_ref[...]).astype(BF16)
    y = jnp.dot(xn, w_ref[...], preferred_element_type=F32)
    w = fq_ref.shape[1]
    fq_ref[...] = y[:, :w]
    fk_ref[...] = y[:, w:2 * w]
    fv_ref[...] = y[:, 2 * w:]
    ff = jnp.dot(xn, wf_ref[...], preferred_element_type=F32)
    lf_ref[...] = _log_sigmoid(ff + bf_ref[...])[:, :lf_ref.shape[1]]


def _rotate(y, cos, sin):
    outs = []
    for h in range(RET_HEADS):
        yh = y[:, h * RET_DIM:(h + 1) * RET_DIM]
        outs.append(yh * cos + pltpu.roll(yh, RET_DIM // 2, axis=1) * sin)
    return jnp.concatenate(outs, axis=1)


def _proj_ret_kernel(x_ref, g_ref, w_ref, wm_ref, cos_ref, sin_ref, rq_ref, rk_ref, rv_ref, rg_ref, mq_ref):
    xn = _rms(x_ref[...], g_ref[...]).astype(BF16)
    y = jnp.dot(xn, w_ref[...], preferred_element_type=F32)
    w = rq_ref.shape[1]
    cos, sin = cos_ref[...], sin_ref[...]
    rq_ref[...] = _rotate(y[:, :w], cos, sin)
    rk_ref[...] = _rotate(y[:, w:2 * w], cos, sin) * (RET_DIM ** -0.5)
    rv_ref[...] = y[:, 2 * w:3 * w]
    rg_ref[...] = y[:, 3 * w:]
    mq_ref[...] = jnp.dot(xn, wm_ref[...], preferred_element_type=F32)


def _input_projection(x, g, wparts, b_forget_pad, cos, sin, tm):
    R, D = x.shape
    w_fox, w_ff, w_ret, w_mem, _ = wparts
    fw = FOX_HEADS * FOX_DIM
    rw = RET_HEADS * RET_DIM
    mw = MEM_HEADS * MEM_DIM
    row = lambda n: pl.BlockSpec((tm, n), lambda i: (i, 0))
    sds = lambda n: jax.ShapeDtypeStruct((R, n), F32)
    grid = (R // tm,)
    fq, fk, fv, lf = pl.pallas_call(
        _proj_fox_kernel, grid=grid,
        in_specs=[row(D), _full((1, D)), _full(w_fox.shape), _full(w_ff.shape), _full((1, 128))],
        out_specs=[row(fw), row(fw), row(fw), row(FOX_HEADS)],
        out_shape=[sds(fw), sds(fw), sds(fw), sds(FOX_HEADS)],
        compiler_params=_params(("parallel",)), name="proj_fox",
    )(x, g, w_fox, w_ff, b_forget_pad)
    rq, rk, rv, rg, mq = pl.pallas_call(
        _proj_ret_kernel, grid=grid,
        in_specs=[row(D), _full((1, D)), _full(w_ret.shape), _full(w_mem.shape), row(RET_DIM), row(RET_DIM)],
        out_specs=[row(rw), row(rw), row(rw), row(rw), row(mw)],
        out_shape=[sds(rw), sds(rw), sds(rw), sds(rw), sds(mw)],
        compiler_params=_params(("parallel",)), name="proj_ret",
    )(x, g, w_ret, w_mem, cos, sin)
    return fq, fk, fv, lf, rq, rk, rv, rg, mq


def _memkv_kernel(x_ref, g_ref, w_ref, mk_ref, mv_ref):
    xn = _rms(x_ref[...], g_ref[...]).astype(BF16)
    y = jnp.dot(xn, w_ref[...], preferred_element_type=F32)
    w = mk_ref.shape[1]
    mk_ref[...] = y[:, :w]
    mv_ref[...] = y[:, w:]


def _memory_kv(mem, g, w):
    M, D = mem.shape
    mw = MEM_HEADS * MEM_DIM
    return pl.pallas_call(
        _memkv_kernel, grid=(1,),
        in_specs=[_full((M, D)), _full((1, D)), _full(w.shape)],
        out_specs=[_full((M, mw)), _full((M, mw))],
        out_shape=[jax.ShapeDtypeStruct((M, mw), F32)] * 2,
        compiler_params=_params(("arbitrary",)), name="memory_kv",
    )(mem, g, w)


def _fox_prep_kernel(fq_ref, fk_ref, fv_ref, lf_ref, tri_ref, qa_ref, ka_ref, vt_ref, carry_ref):
    @pl.when(pl.program_id(0) == 0)
    def _():
        carry_ref[...] = jnp.zeros_like(carry_ref)

    tm = fq_ref.shape[0]
    c = jnp.dot(tri_ref[...], lf_ref[...], preferred_element_type=F32, precision=HI) + carry_ref[...]
    carry_ref[...] = c[tm - 1:tm, :]
    hi, mid, lo = (p.astype(F32) for p in _split3(c * LOG2E))
    lane = lax.broadcasted_iota(I32, (tm, 2 * FOX_DIM), 1)
    d = FOX_DIM
    fq, fk, fv = fq_ref[...], fk_ref[...], fv_ref[...]
    for h in range(FOX_HEADS):
        g = h // 2
        sl = slice(g * 2 * d, (g + 1) * 2 * d)
        qg, kg = fq[:, sl], fk[:, sl]
        if h % 2:
            qg, kg = (pltpu.roll(t, d, axis=1) for t in (qg, kg))
        ch, cm, cl = hi[:, h:h + 1], mid[:, h:h + 1], lo[:, h:h + 1]
        one = jnp.where((lane >= d + 3) & (lane < d + 6), 1.0, 0.0)
        qa = jnp.where(lane < d, qg * (d ** -0.5 * LOG2E),
                       jnp.where(lane == d, ch, jnp.where(lane == d + 1, cm, jnp.where(lane == d + 2, cl, one))))
        onek = jnp.where((lane >= d) & (lane < d + 3), 1.0, 0.0)
        ka = jnp.where(lane < d, kg,
                       jnp.where(lane == d + 3, -ch, jnp.where(lane == d + 4, -cm, jnp.where(lane == d + 5, -cl, onek))))
        qa_ref[h] = qa.astype(BF16)
        ka_ref[h] = ka.astype(BF16)
    vt_ref[...] = fv.T.astype(BF16)


def _fox_prep(fq, fk, fv, lf, tm):
    S, W = fq.shape
    tri = jnp.tril(jnp.ones((tm, tm), F32))
    row = lambda n: pl.BlockSpec((tm, n), lambda i: (i, 0))
    hm = lambda n: pl.BlockSpec((FOX_HEADS, tm, n), lambda i: (0, i, 0))
    return pl.pallas_call(
        _fox_prep_kernel, grid=(S // tm,),
        in_specs=[row(W), row(W), row(W), row(FOX_HEADS), _full((tm, tm))],
        out_specs=[hm(2 * FOX_DIM), hm(2 * FOX_DIM), pl.BlockSpec((W, tm), lambda i: (0, i))],
        out_shape=[jax.ShapeDtypeStruct((FOX_HEADS, S, 2 * FOX_DIM), BF16)] * 2
        + [jax.ShapeDtypeStruct((W, S), BF16)],
        scratch_shapes=[pltpu.VMEM((1, FOX_HEADS), F32)],
        compiler_params=_params(("arbitrary",)), name="fox_prep",
    )(fq, fk, fv, lf, tri)


def _fox_prompt_kernel(q_ref, k_ref, vt_ref, o_ref, s0_ref, s1_ref, m_ref, l_ref, acc_ref, *, t):
    i = pl.program_id(1)
    q = q_ref[...]

    def scores(j, dst):
        k0 = pl.multiple_of(j * t, t)
        dst[...] = lax.dot_general(k_ref[pl.ds(k0, t), :], q, NT, preferred_element_type=F32)

    def update(j, src, masked):
        k0 = pl.multiple_of(j * t, t)
        st = src[...]
        if masked:
            st = jnp.where(lax.broadcasted_iota(I32, (t, t), 0) <= lax.broadcasted_iota(I32, (t, t), 1), st, NEG)
        m = m_ref[...]
        m_new = jnp.maximum(m, jnp.max(st, axis=0, keepdims=True))
        alpha = jnp.exp2(m - m_new)
        p = jnp.exp2(st - m_new)
        l_ref[...] = alpha * l_ref[...] + jnp.sum(p, axis=0, keepdims=True)
        acc_ref[...] = alpha * acc_ref[...] + jnp.dot(vt_ref[:, pl.ds(k0, t)], p.astype(BF16),
                                                      preferred_element_type=F32)
        m_ref[...] = m_new

    m_ref[...] = jnp.full_like(m_ref, NEG)
    l_ref[...] = jnp.zeros_like(l_ref)
    acc_ref[...] = jnp.zeros_like(acc_ref)
    scores(0, s0_ref)

    def body(jp, carry):
        scores(2 * jp + 1, s1_ref)
        update(2 * jp, s0_ref, False)
        scores(2 * jp + 2, s0_ref)
        update(2 * jp + 1, s1_ref, False)
        return carry

    lax.fori_loop(0, i // 2, body, 0)

    @pl.when(i % 2 == 0)
    def _():
        update(i, s0_ref, True)

    @pl.when(i % 2 == 1)
    def _():
        scores(i, s1_ref)
        update(i - 1, s0_ref, False)
        update(i, s1_ref, True)

    o_ref[...] = (acc_ref[...] / l_ref[...]).astype(o_ref.dtype)


def _fox_prompt(qa, ka, vt, t):
    H, S, A = qa.shape
    return pl.pallas_call(
        functools.partial(_fox_prompt_kernel, t=t), grid=(H, S // t),
        in_specs=[pl.BlockSpec((None, t, A), lambda h, i: (h, i, 0)),
                  pl.BlockSpec((None, S, A), lambda h, i: (h, 0, 0)),
                  pl.BlockSpec((FOX_DIM, S), lambda h, i: (h, 0))],
        out_specs=pl.BlockSpec((FOX_DIM, t), lambda h, i: (h, i)),
        out_shape=jax.ShapeDtypeStruct((H * FOX_DIM, S), BF16),
        scratch_shapes=[pltpu.VMEM((t, t), F32), pltpu.VMEM((t, t), F32), pltpu.VMEM((1, t), F32),
                        pltpu.VMEM((1, t), F32), pltpu.VMEM((FOX_DIM, t), F32)],
        compiler_params=_params(("parallel", "arbitrary")), name="fox_prompt",
    )(qa, ka, vt)


def _fox_sample_kernel(pt_ref, q_ref, kn_ref, vn_ref, lfn_ref, lfnt_ref, ms_ref, *rest, pp):
    k_refs, v_refs, l_refs = rest[:pp], rest[pp:2 * pp], rest[2 * pp:3 * pp]
    o_ref = rest[3 * pp]
    m_ref, l_ref, acc_ref, carry_ref, qbd_ref, rowc_ref = rest[3 * pp + 1:]
    j = pl.program_id(1)
    H, d, T = FOX_HEADS, FOX_DIM, q_ref.shape[0]
    HT, W = H * T, H * d
    page = ms_ref.shape[0]

    @pl.when(j == 0)
    def _():
        qsc = q_ref[...] * (d ** -0.5 * LOG2E)
        lfn, lfnt = lfn_ref[...], lfnt_ref[...]
        ti = lax.broadcasted_iota(I32, (T, T), 0)
        tj = lax.broadcasted_iota(I32, (T, T), 1)
        low = jnp.where(tj <= ti, 1.0, 0.0)
        up = jnp.where(ti <= tj, 1.0, 0.0)
        cq = jnp.zeros((T, H), F32)
        cqt = jnp.zeros((H, T), F32)
        for t in range(T):
            cq = cq + low[:, t:t + 1] * lfn[t:t + 1, :]
            cqt = cqt + lfnt[:, t:t + 1] * up[t:t + 1, :]
        bias = jnp.concatenate([cq[:, h:h + 1] - cqt[h:h + 1, :] for h in range(H)], axis=0) * LOG2E
        rowc_ref[...] = jnp.concatenate([cq[:, h:h + 1] for h in range(H)], axis=0) * LOG2E
        r_i = lax.broadcasted_iota(I32, (HT, W), 0)
        c_i = lax.broadcasted_iota(I32, (HT, W), 1)
        qbd = jnp.where(c_i // d == r_i // T, jnp.concatenate([qsc] * H, axis=0), 0.0).astype(BF16)
        qbd_ref[...] = qbd
        s = lax.dot_general(qbd, kn_ref[...].astype(BF16), NT, preferred_element_type=F32) + bias
        rr = lax.broadcasted_iota(I32, (HT, T), 0)
        cc = lax.broadcasted_iota(I32, (HT, T), 1)
        s = jnp.where(cc <= rr % T, s, NEG)
        m0 = jnp.max(s, axis=1, keepdims=True)
        p = jnp.exp2(s - m0)
        m_ref[...] = m0
        l_ref[...] = jnp.sum(p, axis=1, keepdims=True)
        acc_ref[...] = jnp.dot(p.astype(BF16), vn_ref[...].astype(BF16), preferred_element_type=F32)
        carry_ref[...] = jnp.zeros_like(carry_ref)

    qbd = qbd_ref[...]
    rowc = rowc_ref[...]
    ms = ms_ref[...]
    lf = jnp.concatenate([l_refs[r][...] for r in range(pp)], axis=0)
    pieces = jnp.concatenate([pc.astype(F32) for pc in _split3(lf)], axis=0).astype(BF16)
    rr3 = jnp.dot(pieces, ms, preferred_element_type=F32)
    n = pp * H
    rr = rr3[:n] + rr3[n:2 * n] + rr3[2 * n:]
    carry = carry_ref[...]
    biases = []
    for r in range(pp):
        rs = (rr[r * H:(r + 1) * H, :page] + carry[:, 0:1]) * LOG2E
        carry = carry + rr[r * H:(r + 1) * H, page:]
        biases.append(jnp.concatenate([jnp.broadcast_to(rs[h:h + 1, :], (T, page)) for h in range(H)], axis=0))
    carry_ref[...] = carry
    bias = jnp.concatenate(biases, axis=1) + rowc
    kb = jnp.concatenate([k_refs[r][...] for r in range(pp)], axis=1).astype(BF16)
    vb = jnp.concatenate([v_refs[r][...] for r in range(pp)], axis=1).astype(BF16)
    s = jnp.dot(qbd, kb, preferred_element_type=F32) + bias
    m = m_ref[...]
    m_new = jnp.maximum(m, jnp.max(s, axis=1, keepdims=True))
    alpha = jnp.exp2(m - m_new)
    p = jnp.exp2(s - m_new)
    l_ref[...] = alpha * l_ref[...] + jnp.sum(p, axis=1, keepdims=True)
    acc_ref[...] = alpha * acc_ref[...] + lax.dot_general(p.astype(BF16), vb, NT, preferred_element_type=F32)
    m_ref[...] = m_new

    @pl.when(j == pl.num_programs(1) - 1)
    def _():
        acc = acc_ref[...]
        o = jnp.concatenate([acc[h * T:(h + 1) * T, h * d:(h + 1) * d] for h in range(H)], axis=0)
        o_ref[...] = (o / l_ref[...]).reshape(H, T, d)


def _fox_sample(q, kn, vn, lfn, page_table, cache_k, cache_v, cache_lf, pp):
    R, W = q.shape
    DB, NP = page_table.shape
    T = R // DB
    n_pool, page, H, d = cache_k.shape
    ck = jnp.transpose(cache_k, (0, 2, 3, 1)).reshape(n_pool, H * d, page)
    cv = jnp.transpose(cache_v, (0, 2, 3, 1)).reshape(n_pool, H * d, page)
    cl = jnp.transpose(cache_lf, (0, 2, 1))
    lfnt = jnp.swapaxes(lfn.reshape(DB, T, H), 1, 2)
    tt = np.arange(page)
    ms = np.concatenate([tt[:, None] > tt[None, :], np.ones((page, 128), bool)], axis=1)
    ms = jnp.asarray(ms, BF16)
    pt = page_table.reshape(-1).astype(I32)

    def page_spec(shape, r):
        return pl.BlockSpec((None,) + shape, lambda b, j, pt: (pt[b * NP + NP - 1 - (j * pp + r)], 0, 0))

    tok = lambda n: pl.BlockSpec((T, n), lambda b, j, pt: (b, 0))
    in_specs = [tok(W), tok(W), tok(W), tok(H), pl.BlockSpec((None, H, T), lambda b, j, pt: (b, 0, 0)),
                pl.BlockSpec(ms.shape, lambda b, j, pt: (0, 0))]
    in_specs += [page_spec((H * d, page), r) for r in range(pp)]
    in_specs += [page_spec((H * d, page), r) for r in range(pp)]
    in_specs += [page_spec((H, page), r) for r in range(pp)]
    grid_spec = pltpu.PrefetchScalarGridSpec(
        num_scalar_prefetch=1, grid=(DB, NP // pp), in_specs=in_specs,
        out_specs=pl.BlockSpec((H, T, d), lambda b, j, pt: (0, b, 0)),
        scratch_shapes=[pltpu.VMEM((H * T, 1), F32), pltpu.VMEM((H * T, 1), F32), pltpu.VMEM((H * T, W), F32),
                        pltpu.VMEM((H, 128), F32), pltpu.VMEM((H * T, W), BF16), pltpu.VMEM((H * T, 1), F32)])
    return pl.pallas_call(
        functools.partial(_fox_sample_kernel, pp=pp), grid_spec=grid_spec,
        out_shape=jax.ShapeDtypeStruct((H, R, d), F32),
        compiler_params=_params(("parallel", "arbitrary")), name="fox_sample",
    )(pt, q, kn, vn, lfn, lfnt, ms, *([ck] * pp), *([cv] * pp), *([cl] * pp))


def _ret_consts(C):
    lg = jnp.log(1.0 - jnp.exp2(-5.0 - jnp.arange(RET_HEADS, dtype=F32)))
    idx = jnp.arange(C, dtype=F32)
    rel = idx[:, None] - idx[None, :]
    intra = jnp.where(rel >= 0, jnp.exp(jnp.maximum(rel, 0.0)[None] * lg[:, None, None]), 0.0)
    qd = jnp.exp((idx + 1.0)[None, :] * lg[:, None])
    kd = jnp.exp((C - 1.0 - idx)[None, :] * lg[:, None])
    cd = jnp.exp(C * lg)
    rep = lambda t: jnp.broadcast_to(t[:, :, None], t.shape + (RET_DIM,))
    return intra, rep(qd), rep(kd), jnp.broadcast_to(cd[:, None, None], (RET_HEADS, 1, RET_DIM))


def _ret_heads(q, k, v, state_of, intra_ref, qd_ref, kd_ref, cd_ref):
    outs, states = [], []
    for h in range(RET_HEADS):
        sl = slice(h * RET_DIM, (h + 1) * RET_DIM)
        qh, kh, vh = q[:, sl].astype(BF16), k[:, sl], v[:, sl].astype(BF16)
        st = state_of(h)
        a = lax.dot_general(qh, kh.astype(BF16), NT, preferred_element_type=F32) * intra_ref[h]
        o = (jnp.dot(a.astype(BF16), vh, preferred_element_type=F32)
             + jnp.dot(qh, st.astype(BF16), preferred_element_type=F32) * qd_ref[h])
        kdec = (kh * kd_ref[h]).astype(BF16)
        states.append(cd_ref[h] * st + lax.dot_general(kdec, vh, TN, preferred_element_type=F32))
        outs.append(o)
    return jnp.concatenate(outs, axis=1), states


def _ret_prompt_kernel(q_ref, k_ref, v_ref, intra_ref, qd_ref, kd_ref, cd_ref, o_ref, so_ref, st_ref):
    @pl.when(pl.program_id(0) == 0)
    def _():
        st_ref[...] = jnp.zeros_like(st_ref)

    o, states = _ret_heads(q_ref[...], k_ref[...], v_ref[...], lambda h: st_ref[h],
                           intra_ref, qd_ref, kd_ref, cd_ref)
    o_ref[...] = o
    for h in range(RET_HEADS):
        st_ref[h] = states[h]

    @pl.when(pl.program_id(0) == pl.num_programs(0) - 1)
    def _():
        so_ref[...] = st_ref[...]


def _ret_prompt(rq, rk, rv):
    S, W = rq.shape
    C = min(RET_CHUNK, S)
    consts = _ret_consts(C)
    row = pl.BlockSpec((C, W), lambda i: (i, 0))
    st_shape = (RET_HEADS, RET_DIM, RET_DIM)
    return pl.pallas_call(
        _ret_prompt_kernel, grid=(S // C,),
        in_specs=[row, row, row] + [_full(c.shape) for c in consts],
        out_specs=[row, _full(st_shape)],
        out_shape=[jax.ShapeDtypeStruct((S, W), F32), jax.ShapeDtypeStruct(st_shape, F32)],
        scratch_shapes=[pltpu.VMEM(st_shape, F32)],
        compiler_params=_params(("arbitrary",)), name="ret_prompt",
    )(rq, rk, rv, *consts)


SAMPLE_SEQS_PER_STEP = 8


def _ret_sample_kernel(q_ref, k_ref, v_ref, s0_ref, intra_ref, qd_ref, kd_ref, cd_ref, o_ref, so_ref):
    nb = s0_ref.shape[0]
    T = q_ref.shape[0] // nb
    for b in range(nb):
        rows = pl.ds(b * T, T)
        o, states = _ret_heads(q_ref[rows, :], k_ref[rows, :], v_ref[rows, :], lambda h: s0_ref[b, h],
                               intra_ref, qd_ref, kd_ref, cd_ref)
        o_ref[rows, :] = o
        for h in range(RET_HEADS):
            so_ref[b, h] = states[h]


def _ret_sample(rq, rk, rv, state0):
    R, W = rq.shape
    DB = state0.shape[0]
    T = R // DB
    nb = _tile_of(DB, SAMPLE_SEQS_PER_STEP)
    consts = _ret_consts(min(RET_CHUNK, T))
    row = pl.BlockSpec((nb * T, W), lambda b: (b, 0))
    st = pl.BlockSpec((nb, RET_HEADS, RET_DIM, RET_DIM), lambda b: (b, 0, 0, 0))
    return pl.pallas_call(
        _ret_sample_kernel, grid=(DB // nb,),
        in_specs=[row, row, row, st] + [_full(c.shape) for c in consts],
        out_specs=[row, st],
        out_shape=[jax.ShapeDtypeStruct((R, W), F32), jax.ShapeDtypeStruct(state0.shape, F32)],
        compiler_params=_params(("parallel",)), name="ret_sample",
    )(rq, rk, rv, state0, *consts)


def _softmax_rows(s):
    e = jnp.exp(s - jnp.max(s, axis=1, keepdims=True))
    return e / jnp.sum(e, axis=1, keepdims=True)


def _mem_prompt_kernel(q_ref, k_ref, v_ref, o_ref):
    q, k, v = q_ref[...], k_ref[...], v_ref[...]
    outs = []
    for h in range(MEM_HEADS):
        sl = slice(h * MEM_DIM, (h + 1) * MEM_DIM)
        s = lax.dot_general(q[:, sl].astype(BF16), k[:, sl].astype(BF16), NT,
                            preferred_element_type=F32) * (MEM_DIM ** -0.5)
        p = _softmax_rows(s).astype(BF16)
        outs.append(jnp.dot(p, v[:, sl].astype(BF16), preferred_element_type=F32))
    o_ref[...] = jnp.concatenate(outs, axis=1).astype(o_ref.dtype)


def _mem_prompt(mq, mk, mv, tm):
    S, W = mq.shape
    row = pl.BlockSpec((tm, W), lambda i: (i, 0))
    return pl.pallas_call(
        _mem_prompt_kernel, grid=(S // tm,),
        in_specs=[row, _full(mk.shape), _full(mv.shape)],
        out_specs=row, out_shape=jax.ShapeDtypeStruct((S, W), BF16),
        compiler_params=_params(("parallel",)), name="mem_prompt",
    )(mq, mk, mv)


def _mem_sample_kernel(q_ref, k_ref, v_ref, mask_ref, o_ref):
    nb = k_ref.shape[0]
    T = q_ref.shape[0] // nb
    mask = mask_ref[...]
    for b in range(nb):
        q = q_ref[pl.ds(b * T, T), :]
        qm = jnp.concatenate([q[:, h * MEM_DIM:(h + 1) * MEM_DIM] for h in range(MEM_HEADS)], axis=0).astype(BF16)
        s = lax.dot_general(qm, k_ref[b].astype(BF16), NT, preferred_element_type=F32) * (MEM_DIM ** -0.5)
        p = _softmax_rows(s + mask).astype(BF16)
        o = jnp.dot(p, v_ref[b].astype(BF16), preferred_element_type=F32)
        o_ref[pl.ds(b * T, T), :] = jnp.concatenate([o[h * T:(h + 1) * T, :] for h in range(MEM_HEADS)], axis=1)


def _mem_sample(mq, cache_k, cache_v):
    R, W = mq.shape
    DB, M, H, d = cache_k.shape
    T = R // DB
    kf = cache_k.reshape(DB, M * H, d)
    vf = cache_v.reshape(DB, M * H, d)
    mask = np.where((np.arange(M * H) % H)[None, :] == (np.arange(H * T) // T)[:, None], 0.0, NEG)
    mask = jnp.asarray(mask, F32)
    nb = _tile_of(DB, SAMPLE_SEQS_PER_STEP)
    row = pl.BlockSpec((nb * T, W), lambda b: (b, 0))
    kv = pl.BlockSpec((nb, M * H, d), lambda b: (b, 0, 0))
    return pl.pallas_call(
        _mem_sample_kernel, grid=(DB // nb,),
        in_specs=[row, kv, kv, _full(mask.shape)],
        out_specs=row, out_shape=jax.ShapeDtypeStruct((R, W), F32),
        compiler_params=_params(("parallel",)), name="mem_sample",
    )(mq, kf, vf, mask)


def _mixer_out_kernel(x_ref, g_ref, fox_ref, ret_ref, rg_ref, mem_ref, wg_ref, wf_ref, wr_ref, wm_ref, wo_ref, o_ref,
                      *, fox_transposed):
    D = x_ref.shape[1]
    x = x_ref[...]
    gates = jnp.dot(_rms(x, g_ref[...]).astype(BF16), wg_ref[...], preferred_element_type=F32)
    if fox_transposed:
        u_fox = lax.dot_general(fox_ref[...], wf_ref[...].reshape(-1, D), TN, preferred_element_type=F32)
    else:
        u_fox = jnp.dot(fox_ref[0].astype(BF16), wf_ref[0], preferred_element_type=F32)
        for h in range(1, FOX_HEADS):
            u_fox += jnp.dot(fox_ref[h].astype(BF16), wf_ref[h], preferred_element_type=F32)
    ret, rg = ret_ref[...], rg_ref[...]
    ys = []
    for h in range(RET_HEADS):
        sl = slice(h * RET_DIM, (h + 1) * RET_DIM)
        rf = ret[:, sl]
        rn = rf * lax.rsqrt(jnp.mean(rf * rf, axis=1, keepdims=True) + EPS)
        g = rg[:, sl]
        ys.append((rn * (g * _sigmoid(g))).astype(BF16))
    u_ret = jnp.dot(jnp.concatenate(ys, axis=1), wr_ref[...], preferred_element_type=F32)
    u_mem = jnp.dot(mem_ref[...].astype(BF16), wm_ref[...], preferred_element_type=F32)
    merged = (_sigmoid(gates[:, :D]) * u_fox + _sigmoid(gates[:, D:2 * D]) * u_ret
              + _sigmoid(gates[:, 2 * D:]) * u_mem)
    o_ref[...] = x + jnp.dot(merged.astype(BF16), wo_ref[...], preferred_element_type=F32)


def _mixer_out(x, g, fox_o, ret_o, rg, mem_o, w_gate, w_up_fox, w_up_ret, w_up_mem, w_out, tm):
    R, D = x.shape
    row = lambda n: pl.BlockSpec((tm, n), lambda i: (i, 0))
    fox_transposed = fox_o.ndim == 2
    if fox_transposed:
        fox_spec = pl.BlockSpec((fox_o.shape[0], tm), lambda i: (0, i))
    else:
        fox_spec = pl.BlockSpec((FOX_HEADS, tm, FOX_DIM), lambda i: (0, i, 0))
    return pl.pallas_call(
        functools.partial(_mixer_out_kernel, fox_transposed=fox_transposed), grid=(R // tm,),
        in_specs=[row(D), _full((1, D)), fox_spec,
                  row(ret_o.shape[1]), row(rg.shape[1]), row(mem_o.shape[1]), _full(w_gate.shape),
                  _full(w_up_fox.shape), _full(w_up_ret.shape), _full(w_up_mem.shape), _full(w_out.shape)],
        out_specs=row(D), out_shape=jax.ShapeDtypeStruct((R, D), F32),
        compiler_params=_params(("parallel",)), name="mixer_out",
    )(x, g, fox_o, ret_o, rg, mem_o, w_gate, w_up_fox, w_up_ret, w_up_mem, w_out)


def _router_kernel(h_ref, g_ref, wr_ref, br_ref, tri_ref, hn_ref, gate_ref, rank_ref, cnt_ref):
    hn = _rms(h_ref[...], g_ref[...])
    hn_ref[...] = hn.astype(BF16)
    logits = lax.dot_general(wr_ref[...], hn, NT, preferred_element_type=F32, precision=HI) + br_ref[...]
    E, TS = logits.shape
    e_iota = lax.broadcasted_iota(I32, (E, TS), 0)
    work = logits
    vals, hots = [], []
    for _ in range(TOP_K):
        mx = jnp.max(work, axis=0, keepdims=True)
        idx = jnp.min(jnp.where(work == mx, e_iota, E), axis=0, keepdims=True)
        hot = e_iota == idx
        vals.append(mx)
        hots.append(hot)
        work = jnp.where(hot, -jnp.inf, work)
    es = [jnp.exp(v - vals[0]) for v in vals]
    den = es[0]
    for e in es[1:]:
        den = den + e
    gate = jnp.zeros((E, TS), F32)
    sel = jnp.zeros((E, TS), F32)
    for hot, e in zip(hots, es):
        gate = jnp.where(hot, e / den, gate)
        sel = jnp.where(hot, 1.0, sel)
    cum = jnp.dot(sel.astype(BF16), tri_ref[...], preferred_element_type=F32)
    gate_ref[...] = gate
    rank_ref[...] = jnp.where(sel > 0.0, cum - 1.0, -1.0).astype(I32)
    cnt_ref[...] = jnp.broadcast_to(cum[:, TS - 1:TS], cnt_ref.shape).astype(I32)


def _router(h, g, w_router_t, b_router, ts):
    T, D = h.shape
    E = w_router_t.shape[0]
    ns = T // ts
    tri = jnp.triu(jnp.ones((ts, ts), BF16))
    tile = lambda dt: jax.ShapeDtypeStruct((ns, E, ts), dt)
    tspec = pl.BlockSpec((None, E, ts), lambda i: (i, 0, 0))
    return pl.pallas_call(
        _router_kernel, grid=(ns,),
        in_specs=[pl.BlockSpec((ts, D), lambda i: (i, 0)), _full((1, D)), _full((E, D)), _full((E, 1)),
                  _full((ts, ts))],
        out_specs=[pl.BlockSpec((ts, D), lambda i: (i, 0)), tspec, tspec,
                   pl.BlockSpec((None, E, 128), lambda i: (i, 0, 0))],
        out_shape=[jax.ShapeDtypeStruct((T, D), BF16), tile(F32), tile(I32),
                   jax.ShapeDtypeStruct((ns, E, 128), I32)],
        compiler_params=_params(("parallel",)), name="router",
    )(h, g, w_router_t, b_router, tri)


def _moe_kernel(cnt_ref, x_ref, gate_ref, rank_ref, w1_ref, b1_ref, w2_ref, b2_ref, o_ref, yw_ref, *, ns, ts, rb, cb):
    g, e, si = pl.program_id(0), pl.program_id(1), pl.program_id(2)

    @pl.when((e == 0) & (si == 0))
    def _():
        o_ref[...] = jnp.zeros_like(o_ref)
        yw_ref[...] = jnp.zeros_like(yw_ref)

    E = pl.num_programs(1)
    cnt = cnt_ref[(g * ns + si) * E + e]
    F = w2_ref.shape[0]
    t0 = pl.multiple_of(si * ts, ts)

    def block(bi, carry):
        rank = rank_ref[...]
        r0 = pl.multiple_of(bi * rb, rb)
        hit = r0 + lax.broadcasted_iota(I32, (rb, ts), 0) == rank
        onehot = jnp.where(hit, 1.0, 0.0).astype(BF16)
        xb = jnp.dot(onehot, x_ref[pl.ds(t0, ts), :], preferred_element_type=F32).astype(BF16)
        hdn = jnp.dot(xb, w1_ref[...], preferred_element_type=F32) + b1_ref[...]
        glu = jnp.minimum(hdn[:, :F], SWIGLU_LIMIT)
        lin = jnp.clip(hdn[:, F:], -SWIGLU_LIMIT, SWIGLU_LIMIT)
        act = (glu * _sigmoid(SWIGLU_ALPHA * glu) * (lin + 1.0)).astype(BF16)
        y = jnp.dot(act, w2_ref[...], preferred_element_type=F32) + b2_ref[...]
        wrow = jnp.sum(jnp.where(hit, gate_ref[...], 0.0), axis=1, keepdims=True)
        yw_ref[pl.ds(r0, rb), :] = (y * wrow).astype(BF16)
        return carry

    lax.fori_loop(0, (cnt + rb - 1) // rb, block, 0)

    def chunk(ci, carry):
        r0 = pl.multiple_of(ci * cb, cb)
        hit = r0 + lax.broadcasted_iota(I32, (cb, ts), 0) == rank_ref[...]
        onehot = jnp.where(hit, 1.0, 0.0).astype(BF16)
        o_ref[pl.ds(t0, ts), :] += lax.dot_general(onehot, yw_ref[pl.ds(r0, cb), :], TN,
                                                   preferred_element_type=F32)
        return carry

    lax.fori_loop(0, (cnt + cb - 1) // cb, chunk, 0)


def _moe(hn, gate, rank, counts, w1, b1, w2, b2, ts, ns, rb, cb):
    T, D = hn.shape
    E, _, F2 = w1.shape
    ngroups = T // (ts * ns)
    tile = lambda: pl.BlockSpec((None, None, 1, ts), lambda g, e, s, c: (g * ns + s, e, 0, 0))
    grid_spec = pltpu.PrefetchScalarGridSpec(
        num_scalar_prefetch=1, grid=(ngroups, E, ns),
        in_specs=[pl.BlockSpec((ns * ts, D), lambda g, e, s, c: (g, 0)), tile(), tile(),
                  pl.BlockSpec((None, D, F2), lambda g, e, s, c: (e, 0, 0)),
                  pl.BlockSpec((None, 1, F2), lambda g, e, s, c: (e, 0, 0)),
                  pl.BlockSpec((None, F2 // 2, D), lambda g, e, s, c: (e, 0, 0)),
                  pl.BlockSpec((None, 1, D), lambda g, e, s, c: (e, 0, 0))],
        out_specs=pl.BlockSpec((ns * ts, D), lambda g, e, s, c: (g, 0)),
        scratch_shapes=[pltpu.VMEM((pl.cdiv(ts, cb) * cb, D), BF16)])
    nst = T // ts
    return pl.pallas_call(
        functools.partial(_moe_kernel, ns=ns, ts=ts, rb=rb, cb=cb), grid_spec=grid_spec,
        out_shape=jax.ShapeDtypeStruct((T, D), F32),
        compiler_params=_params(("parallel", "arbitrary", "arbitrary")), name="moe",
    )(counts, hn, gate.reshape(nst, E, 1, ts), rank.reshape(nst, E, 1, ts), w1, b1, w2, b2)


def _final_kernel(h_ref, y_ref, g_ref, o_ref):
    o_ref[...] = _rms(h_ref[...] + y_ref[...], g_ref[...])


def _final_norm(h, y, g, tm):
    R, D = h.shape
    row = pl.BlockSpec((tm, D), lambda i: (i, 0))
    return pl.pallas_call(
        _final_kernel, grid=(R // tm,), in_specs=[row, row, _full((1, D))], out_specs=row,
        out_shape=jax.ShapeDtypeStruct((R, D), F32),
        compiler_params=_params(("parallel",)), name="final_norm",
    )(h, y, g)


def _rope_tables(pos):
    half = RET_DIM // 2
    inv_freq = 1.0 / (ROPE_BASE ** jnp.linspace(0.0, 1.0, half, dtype=F32))
    ang = pos.astype(F32)[:, None] * inv_freq[None, :]
    cos, sin = jnp.cos(ang), jnp.sin(ang)
    return jnp.concatenate([cos, cos], axis=1), jnp.concatenate([-sin, sin], axis=1)


def _tile_of(n, pref):
    t = min(n, pref)
    assert n % t == 0, (n, t)
    return t


MOE_ROW_BLOCK = 128
MOE_SCATTER_BLOCK = 256


def _moe_layer(h, lw, ts, ns):
    g_moe, w_router_t, b_router, w1, b1, w2, b2 = lw
    hn, gate, rank, cnt = _router(h, g_moe, w_router_t, b_router, ts)
    counts = cnt[:, :, 0].reshape(-1)
    return _moe(hn, gate, rank, counts, w1, b1, w2, b2, ts, ns, MOE_ROW_BLOCK, MOE_SCATTER_BLOCK)


def kernel(x_prompt, x_sample, mem_prompt, cache_fox_k, cache_fox_v, cache_fox_logf, state_ret, cache_mem_k, cache_mem_v, page_table, g_mix, w_in, b_forget, w_up_fox, w_up_ret, w_up_mem, w_out, g_mem, w_mem_kv, g_moe, w_router, b_router, w_ff1, b_ff1, w_ff2, b_ff2, g_final):
    B, S, D = x_prompt.shape
    DB, T, _ = x_sample.shape
    depth = g_mix.shape[0]
    assert B == 1 and depth == 1
    NP = page_table.shape[1]
    page = cache_fox_k.shape[2]
    past_len = NP * page
    fw, rw, mw = FOX_HEADS * FOX_DIM, RET_HEADS * RET_DIM, MEM_HEADS * MEM_DIM
    R = DB * T
    l = 0

    wi = w_in[l]
    c0 = 3 * fw
    c1 = c0 + FOX_HEADS
    c2 = c1 + 4 * rw
    c3 = c2 + mw
    w_ffp = jnp.zeros((D, 128), F32).at[:, :FOX_HEADS].set(wi[:, c0:c1])
    wparts = tuple(t.astype(BF16) for t in (wi[:, :c0], w_ffp, wi[:, c1:c2], wi[:, c2:c3], wi[:, c3:]))
    bfp = jnp.zeros((1, 128), F32).at[0, :FOX_HEADS].set(b_forget[l])
    gm = g_mix[l].reshape(1, D)
    wuf = w_up_fox[l].astype(BF16).reshape(FOX_HEADS, FOX_DIM, D)
    wur, wum, wo = (t[l].astype(BF16) for t in (w_up_ret, w_up_mem, w_out))
    moe_w = (g_moe[l].reshape(1, D), jnp.swapaxes(w_router[l], 0, 1), b_router[l].reshape(-1, 1),
             w_ff1[l].astype(BF16), b_ff1[l][:, None, :], w_ff2[l].astype(BF16), b_ff2[l][:, None, :])
    gf = g_final.reshape(1, D)

    xp = x_prompt.reshape(S, D)
    tmp = _tile_of(S, 512)
    cos_p, sin_p = _rope_tables(jnp.arange(S))
    mk, mv = _memory_kv(mem_prompt[0], g_mem[l].reshape(1, D), w_mem_kv[l].astype(BF16))
    fq, fk, fv, lf, rq, rk, rv, rg, mq = _input_projection(xp, gm, wparts, bfp, cos_p, sin_p, tmp)
    qa, ka, vh = _fox_prep(fq, fk, fv, lf, tmp)
    fox_o = _fox_prompt(qa, ka, vh, _tile_of(S, 512))
    ret_o, rs_p = _ret_prompt(rq, rk, rv)
    mem_o = _mem_prompt(mq, mk, mv, tmp)
    hp = _mixer_out(xp, gm, fox_o, ret_o, rg, mem_o, wparts[4], wuf, wur, wum, wo, _tile_of(S, 256))
    ts = _tile_of(S, 1024)
    yp = _moe_layer(hp, moe_w, ts, _tile_of(S // ts, 2))
    y_prompt = _final_norm(hp, yp, gf, tmp)

    xs = x_sample.reshape(R, D)
    tms = _tile_of(R, 512)
    cos_s, sin_s = _rope_tables(past_len + jnp.arange(T))
    cos_s, sin_s = jnp.tile(cos_s, (DB, 1)), jnp.tile(sin_s, (DB, 1))
    fq_s, fk_s, fv_s, lf_s, rq_s, rk_s, rv_s, rg_s, mq_s = _input_projection(
        xs, gm, wparts, bfp, cos_s, sin_s, tms)
    fox_os = _fox_sample(fq_s, fk_s, fv_s, lf_s, page_table, cache_fox_k[l], cache_fox_v[l], cache_fox_logf[l],
                         _tile_of(NP, 16))
    ret_os, rs_s = _ret_sample(rq_s, rk_s, rv_s, state_ret[l])
    mem_os = _mem_sample(mq_s, cache_mem_k[l], cache_mem_v[l])
    hs = _mixer_out(xs, gm, fox_os, ret_os, rg_s, mem_os, wparts[4], wuf, wur, wum, wo, _tile_of(R, 256))
    tss = _tile_of(R, 1024)
    ys = _moe_layer(hs, moe_w, tss, 1)
    y_sample = _final_norm(hs, ys, gf, tms)

    return (y_prompt.reshape(B, S, D), y_sample.reshape(DB, T, D),
            fk.reshape(1, B, S, FOX_HEADS, FOX_DIM), fv.reshape(1, B, S, FOX_HEADS, FOX_DIM),
            lf.reshape(1, B, S, FOX_HEADS), rs_p.reshape(1, B, RET_HEADS, RET_DIM, RET_DIM),
            mk.reshape(1, B, -1, MEM_HEADS, MEM_DIM), mv.reshape(1, B, -1, MEM_HEADS, MEM_DIM),
            fk_s.reshape(1, DB, T, FOX_HEADS, FOX_DIM), fv_s.reshape(1, DB, T, FOX_HEADS, FOX_DIM),
            lf_s.reshape(1, DB, T, FOX_HEADS), rs_s.reshape(1, DB, RET_HEADS, RET_DIM, RET_DIM))
```

```python
import functools
import math

import jax
import jax.numpy as jnp
import numpy as np
from jax import lax
from jax.experimental import pallas as pl
from jax.experimental.pallas import tpu as pltpu

F32 = jnp.float32
BF16 = jnp.bfloat16
I32 = jnp.int32

FOX_HEADS = 8
FOX_DIM = 64
RET_HEADS = 4
RET_DIM = 128
RET_CHUNK = 128
MEM_HEADS = 4
MEM_DIM = 128
N_BRANCH = 3
TOP_K = 4
ROPE_BASE = 10000.0
SWIGLU_LIMIT = 7.0
SWIGLU_ALPHA = 1.702
EPS = 1e-6

LOG2E = 1.4426950408889634
NEG = -1e30
VMEM_LIMIT = 48 * 1024 * 1024

HI = lax.Precision.HIGHEST
NT = (((1,), (1,)), ((), ()))
TN = (((0,), (0,)), ((), ()))


def _params(sem, vmem=VMEM_LIMIT):
    return pltpu.CompilerParams(dimension_semantics=sem, vmem_limit_bytes=vmem)


def _rms(x, g):
    return x * lax.rsqrt(jnp.mean(x * x, axis=-1, keepdims=True) + EPS) * g


def _log_sigmoid(x):
    return jnp.minimum(x, 0.0) - jnp.log(1.0 + jnp.exp(-jnp.abs(x)))


def _sigmoid(x):
    return 1.0 / (1.0 + jnp.exp(-x))


def _split3(x):
    hi = x.astype(BF16)
    r = x - hi.astype(F32)
    mid = r.astype(BF16)
    lo = (r - mid.astype(F32)).astype(BF16)
    return hi, mid, lo


def _full(shape):
    nd = len(shape)
    return pl.BlockSpec(shape, lambda *_: (0,) * nd)


def _proj_fox_kernel(x_ref, g_ref, w_ref, wf_ref, bf_ref, fq_ref, fk_ref, fv_ref, lf_ref):
    xn = _rms(x_ref[...], g_ref[...]).astype(BF16)
    y = jnp.dot(xn, w_ref[...], preferred_element_type=F32)
    w = fq_ref.shape[1]
    fq_ref[...] = y[:, :w]
    fk_ref[...] = y[:, w:2 * w]
    fv_ref[...] = y[:, 2 * w:]
    ff = jnp.dot(xn, wf_ref[...], preferred_element_type=F32)
    lf_ref[...] = _log_sigmoid(ff + bf_ref[...])[:, :lf_ref.shape[1]]


def _rotate(y, cos, sin):
    outs = []
    for h in range(RET_HEADS):
        yh = y[:, h * RET_DIM:(h + 1) * RET_DIM]
        outs.append(yh * cos + pltpu.roll(yh, RET_DIM // 2, axis=1) * sin)
    return jnp.concatenate(outs, axis=1)


def _proj_ret_kernel(x_ref, g_ref, w_ref, wm_ref, cos_ref, sin_ref, rq_ref, rk_ref, rv_ref, rg_ref, mq_ref):
    xn = _rms(x_ref[...], g_ref[...]).astype(BF16)
    y = jnp.dot(xn, w_ref[...], preferred_element_type=F32)
    w = rq_ref.shape[1]
    cos, sin = cos_ref[...], sin_ref[...]
    rq_ref[...] = _rotate(y[:, :w], cos, sin)
    rk_ref[...] = _rotate(y[:, w:2 * w], cos, sin) * (RET_DIM ** -0.5)
    rv_ref[...] = y[:, 2 * w:3 * w]
    rg_ref[...] = y[:, 3 * w:]
    mq_ref[...] = jnp.dot(xn, wm_ref[...], preferred_element_type=F32)


def _input_projection(x, g, wparts, b_forget_pad, cos, sin, tm):
    R, D = x.shape
    w_fox, w_ff, w_ret, w_mem, _ = wparts
    fw = FOX_HEADS * FOX_DIM
    rw = RET_HEADS * RET_DIM
    mw = MEM_HEADS * MEM_DIM
    row = lambda n: pl.BlockSpec((tm, n), lambda i: (i, 0))
    sds = lambda n: jax.ShapeDtypeStruct((R, n), F32)
    grid = (R // tm,)
    fq, fk, fv, lf = pl.pallas_call(
        _proj_fox_kernel, grid=grid,
        in_specs=[row(D), _full((1, D)), _full(w_fox.shape), _full(w_ff.shape), _full((1, 128))],
        out_specs=[row(fw), row(fw), row(fw), row(FOX_HEADS)],
        out_shape=[sds(fw), sds(fw), sds(fw), sds(FOX_HEADS)],
        compiler_params=_params(("parallel",)), name="proj_fox",
    )(x, g, w_fox, w_ff, b_forget_pad)
    rq, rk, rv, rg, mq = pl.pallas_call(
        _proj_ret_kernel, grid=grid,
        in_specs=[row(D), _full((1, D)), _full(w_ret.shape), _full(w_mem.shape), row(RET_DIM), row(RET_DIM)],
        out_specs=[row(rw), row(rw), row(rw), row(rw), row(mw)],
        out_shape=[sds(rw), sds(rw), sds(rw), sds(rw), sds(mw)],
        compiler_params=_params(("parallel",)), name="proj_ret",
    )(x, g, w_ret, w_mem, cos, sin)
    return fq, fk, fv, lf, rq, rk, rv, rg, mq


def _memkv_kernel(x_ref, g_ref, w_ref, mk_ref, mv_ref):
    xn = _rms(x_ref[...], g_ref[...]).astype(BF16)
    y = jnp.dot(xn, w_ref[...], preferred_element_type=F32)
    w = mk_ref.shape[1]
    mk_ref[...] = y[:, :w]
    mv_ref[...] = y[:, w:]


def _memory_kv(mem, g, w):
    M, D = mem.shape
    mw = MEM_HEADS * MEM_DIM
    return pl.pallas_call(
        _memkv_kernel, grid=(1,),
        in_specs=[_full((M, D)), _full((1, D)), _full(w.shape)],
        out_specs=[_full((M, mw)), _full((M, mw))],
        out_shape=[jax.ShapeDtypeStruct((M, mw), F32)] * 2,
        compiler_params=_params(("arbitrary",)), name="memory_kv",
    )(mem, g, w)


def _fox_prep_kernel(fq_ref, fk_ref, fv_ref, lf_ref, tri_ref, qa_ref, ka_ref, vt_ref, carry_ref):
    @pl.when(pl.program_id(0) == 0)
    def _():
        carry_ref[...] = jnp.zeros_like(carry_ref)

    tm = fq_ref.shape[0]
    c = jnp.dot(tri_ref[...], lf_ref[...], preferred_element_type=F32, precision=HI) + carry_ref[...]
    carry_ref[...] = c[tm - 1:tm, :]
    hi, mid, lo = (p.astype(F32) for p in _split3(c * LOG2E))
    lane = lax.broadcasted_iota(I32, (tm, 2 * FOX_DIM), 1)
    d = FOX_DIM
    fq, fk, fv = fq_ref[...], fk_ref[...], fv_ref[...]
    for h in range(FOX_HEADS):
        g = h // 2
        sl = slice(g * 2 * d, (g + 1) * 2 * d)
        qg, kg = fq[:, sl], fk[:, sl]
        if h % 2:
            qg, kg = (pltpu.roll(t, d, axis=1) for t in (qg, kg))
        ch, cm, cl = hi[:, h:h + 1], mid[:, h:h + 1], lo[:, h:h + 1]
        one = jnp.where((lane >= d + 3) & (lane < d + 6), 1.0, 0.0)
        qa = jnp.where(lane < d, qg * (d ** -0.5 * LOG2E),
                       jnp.where(lane == d, ch, jnp.where(lane == d + 1, cm, jnp.where(lane == d + 2, cl, one))))
        onek = jnp.where((lane >= d) & (lane < d + 3), 1.0, 0.0)
        ka = jnp.where(lane < d, kg,
                       jnp.where(lane == d + 3, -ch, jnp.where(lane == d + 4, -cm, jnp.where(lane == d + 5, -cl, onek))))
        qa_ref[h] = qa.astype(BF16)
        ka_ref[h] = ka.astype(BF16)
    vt_ref[...] = fv.T.astype(BF16)


def _fox_prep(fq, fk, fv, lf, tm):
    S, W = fq.shape
    tri = jnp.tril(jnp.ones((tm, tm), F32))
    row = lambda n: pl.BlockSpec((tm, n), lambda i: (i, 0))
    hm = lambda n: pl.BlockSpec((FOX_HEADS, tm, n), lambda i: (0, i, 0))
    return pl.pallas_call(
        _fox_prep_kernel, grid=(S // tm,),
        in_specs=[row(W), row(W), row(W), row(FOX_HEADS), _full((tm, tm))],
        out_specs=[hm(2 * FOX_DIM), hm(2 * FOX_DIM), pl.BlockSpec((W, tm), lambda i: (0, i))],
        out_shape=[jax.ShapeDtypeStruct((FOX_HEADS, S, 2 * FOX_DIM), BF16)] * 2
        + [jax.ShapeDtypeStruct((W, S), BF16)],
        scratch_shapes=[pltpu.VMEM((1, FOX_HEADS), F32)],
        compiler_params=_params(("arbitrary",)), name="fox_prep",
    )(fq, fk, fv, lf, tri)


def _fox_prompt_kernel(q_ref, k_ref, vt_ref, o_ref, s0_ref, s1_ref, m_ref, l_ref, acc_ref, *, t):
    i = pl.program_id(1)
    q = q_ref[...]

    def scores(j, dst):
        k0 = pl.multiple_of(j * t, t)
        dst[...] = lax.dot_general(k_ref[pl.ds(k0, t), :], q, NT, preferred_element_type=F32)

    def update(j, src, masked):
        k0 = pl.multiple_of(j * t, t)
        st = src[...]
        if masked:
            st = jnp.where(lax.broadcasted_iota(I32, (t, t), 0) <= lax.broadcasted_iota(I32, (t, t), 1), st, NEG)
        m = m_ref[...]
        m_new = jnp.maximum(m, jnp.max(st, axis=0, keepdims=True))
        alpha = jnp.exp2(m - m_new)
        p = jnp.exp2(st - m_new)
        l_ref[...] = alpha * l_ref[...] + jnp.sum(p, axis=0, keepdims=True)
        acc_ref[...] = alpha * acc_ref[...] + jnp.dot(vt_ref[:, pl.ds(k0, t)], p.astype(BF16),
                                                      preferred_element_type=F32)
        m_ref[...] = m_new

    m_ref[...] = jnp.full_like(m_ref, NEG)
    l_ref[...] = jnp.zeros_like(l_ref)
    acc_ref[...] = jnp.zeros_like(acc_ref)
    scores(0, s0_ref)

    def body(jp, carry):
        scores(2 * jp + 1, s1_ref)
        update(2 * jp, s0_ref, False)
        scores(2 * jp + 2, s0_ref)
        update(2 * jp + 1, s1_ref, False)
        return carry

    lax.fori_loop(0, i // 2, body, 0)

    @pl.when(i % 2 == 0)
    def _():
        update(i, s0_ref, True)

    @pl.when(i % 2 == 1)
    def _():
        scores(i, s1_ref)
        update(i - 1, s0_ref, False)
        update(i, s1_ref, True)

    o_ref[...] = (acc_ref[...] / l_ref[...]).astype(o_ref.dtype)


def _fox_prompt(qa, ka, vt, t):
    H, S, A = qa.shape
    return pl.pallas_call(
        functools.partial(_fox_prompt_kernel, t=t), grid=(H, S // t),
        in_specs=[pl.BlockSpec((None, t, A), lambda h, i: (h, i, 0)),
                  pl.BlockSpec((None, S, A), lambda h, i: (h, 0, 0)),
                  pl.BlockSpec((FOX_DIM, S), lambda h, i: (h, 0))],
        out_specs=pl.BlockSpec((FOX_DIM, t), lambda h, i: (h, i)),
        out_shape=jax.ShapeDtypeStruct((H * FOX_DIM, S), BF16),
        scratch_shapes=[pltpu.VMEM((t, t), F32), pltpu.VMEM((t, t), F32), pltpu.VMEM((1, t), F32),
                        pltpu.VMEM((1, t), F32), pltpu.VMEM((FOX_DIM, t), F32)],
        compiler_params=_params(("parallel", "arbitrary")), name="fox_prompt",
    )(qa, ka, vt)


def _fox_sample_kernel(pt_ref, q_ref, kn_ref, vn_ref, lfn_ref, lfnt_ref, ms_ref, *rest, pp):
    k_refs, v_refs, l_refs = rest[:pp], rest[pp:2 * pp], rest[2 * pp:3 * pp]
    o_ref = rest[3 * pp]
    m_ref, l_ref, acc_ref, carry_ref, qbd_ref, rowc_ref = rest[3 * pp + 1:]
    j = pl.program_id(1)
    H, d, T = FOX_HEADS, FOX_DIM, q_ref.shape[0]
    HT, W = H * T, H * d
    page = ms_ref.shape[0]

    @pl.when(j == 0)
    def _():
        qsc = q_ref[...] * (d ** -0.5 * LOG2E)
        lfn, lfnt = lfn_ref[...], lfnt_ref[...]
        ti = lax.broadcasted_iota(I32, (T, T), 0)
        tj = lax.broadcasted_iota(I32, (T, T), 1)
        low = jnp.where(tj <= ti, 1.0, 0.0)
        up = jnp.where(ti <= tj, 1.0, 0.0)
        cq = jnp.zeros((T, H), F32)
        cqt = jnp.zeros((H, T), F32)
        for t in range(T):
            cq = cq + low[:, t:t + 1] * lfn[t:t + 1, :]
            cqt = cqt + lfnt[:, t:t + 1] * up[t:t + 1, :]
        bias = jnp.concatenate([cq[:, h:h + 1] - cqt[h:h + 1, :] for h in range(H)], axis=0) * LOG2E
        rowc_ref[...] = jnp.concatenate([cq[:, h:h + 1] for h in range(H)], axis=0) * LOG2E
        r_i = lax.broadcasted_iota(I32, (HT, W), 0)
        c_i = lax.broadcasted_iota(I32, (HT, W), 1)
        qbd = jnp.where(c_i // d == r_i // T, jnp.concatenate([qsc] * H, axis=0), 0.0).astype(BF16)
        qbd_ref[...] = qbd
        s = lax.dot_general(qbd, kn_ref[...].astype(BF16), NT, preferred_element_type=F32) + bias
        rr = lax.broadcasted_iota(I32, (HT, T), 0)
        cc = lax.broadcasted_iota(I32, (HT, T), 1)
        s = jnp.where(cc <= rr % T, s, NEG)
        m0 = jnp.max(s, axis=1, keepdims=True)
        p = jnp.exp2(s - m0)
        m_ref[...] = m0
        l_ref[...] = jnp.sum(p, axis=1, keepdims=True)
        acc_ref[...] = jnp.dot(p.astype(BF16), vn_ref[...].astype(BF16), preferred_element_type=F32)
        carry_ref[...] = jnp.zeros_like(carry_ref)

    qbd = qbd_ref[...]
    rowc = rowc_ref[...]
    ms = ms_ref[...]
    lf = jnp.concatenate([l_refs[r][...] for r in range(pp)], axis=0)
    pieces = jnp.concatenate([pc.astype(F32) for pc in _split3(lf)], axis=0).astype(BF16)
    rr3 = jnp.dot(pieces, ms, preferred_element_type=F32)
    n = pp * H
    rr = rr3[:n] + rr3[n:2 * n] + rr3[2 * n:]
    carry = carry_ref[...]
    biases = []
    for r in range(pp):
        rs = (rr[r * H:(r + 1) * H, :page] + carry[:, 0:1]) * LOG2E
        carry = carry + rr[r * H:(r + 1) * H, page:]
        biases.append(jnp.concatenate([jnp.broadcast_to(rs[h:h + 1, :], (T, page)) for h in range(H)], axis=0))
    carry_ref[...] = carry
    bias = jnp.concatenate(biases, axis=1) + rowc
    kb = jnp.concatenate([k_refs[r][...] for r in range(pp)], axis=1).astype(BF16)
    vb = jnp.concatenate([v_refs[r][...] for r in range(pp)], axis=1).astype(BF16)
    s = jnp.dot(qbd, kb, preferred_element_type=F32) + bias
    m = m_ref[...]
    m_new = jnp.maximum(m, jnp.max(s, axis=1, keepdims=True))
    alpha = jnp.exp2(m - m_new)
    p = jnp.exp2(s - m_new)
    l_ref[...] = alpha * l_ref[...] + jnp.sum(p, axis=1, keepdims=True)
    acc_ref[...] = alpha * acc_ref[...] + lax.dot_general(p.astype(BF16), vb, NT, preferred_element_type=F32)
    m_ref[...] = m_new

    @pl.when(j == pl.num_programs(1) - 1)
    def _():
        acc = acc_ref[...]
        o = jnp.concatenate([acc[h * T:(h + 1) * T, h * d:(h + 1) * d] for h in range(H)], axis=0)
        o_ref[...] = (o / l_ref[...]).reshape(H, T, d)


def _fox_sample(q, kn, vn, lfn, page_table, cache_k, cache_v, cache_lf, pp):
    R, W = q.shape
    DB, NP = page_table.shape
    T = R // DB
    n_pool, page, H, d = cache_k.shape
    ck = jnp.transpose(cache_k, (0, 2, 3, 1)).reshape(n_pool, H * d, page)
    cv = jnp.transpose(cache_v, (0, 2, 3, 1)).reshape(n_pool, H * d, page)
    cl = jnp.transpose(cache_lf, (0, 2, 1))
    lfnt = jnp.swapaxes(lfn.reshape(DB, T, H), 1, 2)
    tt = np.arange(page)
    ms = np.concatenate([tt[:, None] > tt[None, :], np.ones((page, 128), bool)], axis=1)
    ms = jnp.asarray(ms, BF16)
    pt = page_table.reshape(-1).astype(I32)

    def page_spec(shape, r):
        return pl.BlockSpec((None,) + shape, lambda b, j, pt: (pt[b * NP + NP - 1 - (j * pp + r)], 0, 0))

    tok = lambda n: pl.BlockSpec((T, n), lambda b, j, pt: (b, 0))
    in_specs = [tok(W), tok(W), tok(W), tok(H), pl.BlockSpec((None, H, T), lambda b, j, pt: (b, 0, 0)),
                pl.BlockSpec(ms.shape, lambda b, j, pt: (0, 0))]
    in_specs += [page_spec((H * d, page), r) for r in range(pp)]
    in_specs += [page_spec((H * d, page), r) for r in range(pp)]
    in_specs += [page_spec((H, page), r) for r in range(pp)]
    grid_spec = pltpu.PrefetchScalarGridSpec(
        num_scalar_prefetch=1, grid=(DB, NP // pp), in_specs=in_specs,
        out_specs=pl.BlockSpec((H, T, d), lambda b, j, pt: (0, b, 0)),
        scratch_shapes=[pltpu.VMEM((H * T, 1), F32), pltpu.VMEM((H * T, 1), F32), pltpu.VMEM((H * T, W), F32),
                        pltpu.VMEM((H, 128), F32), pltpu.VMEM((H * T, W), BF16), pltpu.VMEM((H * T, 1), F32)])
    return pl.pallas_call(
        functools.partial(_fox_sample_kernel, pp=pp), grid_spec=grid_spec,
        out_shape=jax.ShapeDtypeStruct((H, R, d), F32),
        compiler_params=_params(("parallel", "arbitrary")), name="fox_sample",
    )(pt, q, kn, vn, lfn, lfnt, ms, *([ck] * pp), *([cv] * pp), *([cl] * pp))


def _ret_consts(C):
    lg = jnp.log(1.0 - jnp.exp2(-5.0 - jnp.arange(RET_HEADS, dtype=F32)))
    idx = jnp.arange(C, dtype=F32)
    rel = idx[:, None] - idx[None, :]
    intra = jnp.where(rel >= 0, jnp.exp(jnp.maximum(rel, 0.0)[None] * lg[:, None, None]), 0.0)
    qd = jnp.exp((idx + 1.0)[None, :] * lg[:, None])
    kd = jnp.exp((C - 1.0 - idx)[None, :] * lg[:, None])
    cd = jnp.exp(C * lg)
    rep = lambda t: jnp.broadcast_to(t[:, :, None], t.shape + (RET_DIM,))
    return intra, rep(qd), rep(kd), jnp.broadcast_to(cd[:, None, None], (RET_HEADS, 1, RET_DIM))


def _ret_heads(q, k, v, state_of, intra_ref, qd_ref, kd_ref, cd_ref):
    outs, states = [], []
    for h in range(RET_HEADS):
        sl = slice(h * RET_DIM, (h + 1) * RET_DIM)
        qh, kh, vh = q[:, sl].astype(BF16), k[:, sl], v[:, sl].astype(BF16)
        st = state_of(h)
        a = lax.dot_general(qh, kh.astype(BF16), NT, preferred_element_type=F32) * intra_ref[h]
        o = (jnp.dot(a.astype(BF16), vh, preferred_element_type=F32)
             + jnp.dot(qh, st.astype(BF16), preferred_element_type=F32) * qd_ref[h])
        kdec = (kh * kd_ref[h]).astype(BF16)
        states.append(cd_ref[h] * st + lax.dot_general(kdec, vh, TN, preferred_element_type=F32))
        outs.append(o)
    return jnp.concatenate(outs, axis=1), states


def _ret_prompt_kernel(q_ref, k_ref, v_ref, intra_ref, qd_ref, kd_ref, cd_ref, o_ref, so_ref, st_ref):
    @pl.when(pl.program_id(0) == 0)
    def _():
        st_ref[...] = jnp.zeros_like(st_ref)

    o, states = _ret_heads(q_ref[...], k_ref[...], v_ref[...], lambda h: st_ref[h],
                           intra_ref, qd_ref, kd_ref, cd_ref)
    o_ref[...] = o
    for h in range(RET_HEADS):
        st_ref[h] = states[h]

    @pl.when(pl.program_id(0) == pl.num_programs(0) - 1)
    def _():
        so_ref[...] = st_ref[...]


def _ret_prompt(rq, rk, rv):
    S, W = rq.shape
    C = min(RET_CHUNK, S)
    consts = _ret_consts(C)
    row = pl.BlockSpec((C, W), lambda i: (i, 0))
    st_shape = (RET_HEADS, RET_DIM, RET_DIM)
    return pl.pallas_call(
        _ret_prompt_kernel, grid=(S // C,),
        in_specs=[row, row, row] + [_full(c.shape) for c in consts],
        out_specs=[row, _full(st_shape)],
        out_shape=[jax.ShapeDtypeStruct((S, W), F32), jax.ShapeDtypeStruct(st_shape, F32)],
        scratch_shapes=[pltpu.VMEM(st_shape, F32)],
        compiler_params=_params(("arbitrary",)), name="ret_prompt",
    )(rq, rk, rv, *consts)


SAMPLE_SEQS_PER_STEP = 8


def _ret_sample_kernel(q_ref, k_ref, v_ref, s0_ref, intra_ref, qd_ref, kd_ref, cd_ref, o_ref, so_ref):
    nb = s0_ref.shape[0]
    T = q_ref.shape[0] // nb
    for b in range(nb):
        rows = pl.ds(b * T, T)
        o, states = _ret_heads(q_ref[rows, :], k_ref[rows, :], v_ref[rows, :], lambda h: s0_ref[b, h],
                               intra_ref, qd_ref, kd_ref, cd_ref)
        o_ref[rows, :] = o
        for h in range(RET_HEADS):
            so_ref[b, h] = states[h]


def _ret_sample(rq, rk, rv, state0):
    R, W = rq.shape
    DB = state0.shape[0]
    T = R // DB
    nb = _tile_of(DB, SAMPLE_SEQS_PER_STEP)
    consts = _ret_consts(min(RET_CHUNK, T))
    row = pl.BlockSpec((nb * T, W), lambda b: (b, 0))
    st = pl.BlockSpec((nb, RET_HEADS, RET_DIM, RET_DIM), lambda b: (b, 0, 0, 0))
    return pl.pallas_call(
        _ret_sample_kernel, grid=(DB // nb,),
        in_specs=[row, row, row, st] + [_full(c.shape) for c in consts],
        out_specs=[row, st],
        out_shape=[jax.ShapeDtypeStruct((R, W), F32), jax.ShapeDtypeStruct(state0.shape, F32)],
        compiler_params=_params(("parallel",)), name="ret_sample",
    )(rq, rk, rv, state0, *consts)


def _softmax_rows(s):
    e = jnp.exp(s - jnp.max(s, axis=1, keepdims=True))
    return e / jnp.sum(e, axis=1, keepdims=True)


def _mem_prompt_kernel(q_ref, k_ref, v_ref, o_ref):
    q, k, v = q_ref[...], k_ref[...], v_ref[...]
    outs = []
    for h in range(MEM_HEADS):
        sl = slice(h * MEM_DIM, (h + 1) * MEM_DIM)
        s = lax.dot_general(q[:, sl].astype(BF16), k[:, sl].astype(BF16), NT,
                            preferred_element_type=F32) * (MEM_DIM ** -0.5)
        p = _softmax_rows(s).astype(BF16)
        outs.append(jnp.dot(p, v[:, sl].astype(BF16), preferred_element_type=F32))
    o_ref[...] = jnp.concatenate(outs, axis=1).astype(o_ref.dtype)


def _mem_prompt(mq, mk, mv, tm):
    S, W = mq.shape
    row = pl.BlockSpec((tm, W), lambda i: (i, 0))
    return pl.pallas_call(
        _mem_prompt_kernel, grid=(S // tm,),
        in_specs=[row, _full(mk.shape), _full(mv.shape)],
        out_specs=row, out_shape=jax.ShapeDtypeStruct((S, W), BF16),
        compiler_params=_params(("parallel",)), name="mem_prompt",
    )(mq, mk, mv)


def _mem_sample_kernel(q_ref, k_ref, v_ref, mask_ref, o_ref):
    nb = k_ref.shape[0]
    T = q_ref.shape[0] // nb
    mask = mask_ref[...]
    for b in range(nb):
        q = q_ref[pl.ds(b * T, T), :]
        qm = jnp.concatenate([q[:, h * MEM_DIM:(h + 1) * MEM_DIM] for h in range(MEM_HEADS)], axis=0).astype(BF16)
        s = lax.dot_general(qm, k_ref[b].astype(BF16), NT, preferred_element_type=F32) * (MEM_DIM ** -0.5)
        p = _softmax_rows(s + mask).astype(BF16)
        o = jnp.dot(p, v_ref[b].astype(BF16), preferred_element_type=F32)
        o_ref[pl.ds(b * T, T), :] = jnp.concatenate([o[h * T:(h + 1) * T, :] for h in range(MEM_HEADS)], axis=1)


def _mem_sample(mq, cache_k, cache_v):
    R, W = mq.shape
    DB, M, H, d = cache_k.shape
    T = R // DB
    kf = cache_k.reshape(DB, M * H, d)
    vf = cache_v.reshape(DB, M * H, d)
    mask = np.where((np.arange(M * H) % H)[None, :] == (np.arange(H * T) // T)[:, None], 0.0, NEG)
    mask = jnp.asarray(mask, F32)
    nb = _tile_of(DB, SAMPLE_SEQS_PER_STEP)
    row = pl.BlockSpec((nb * T, W), lambda b: (b, 0))
    kv = pl.BlockSpec((nb, M * H, d), lambda b: (b, 0, 0))
    return pl.pallas_call(
        _mem_sample_kernel, grid=(DB // nb,),
        in_specs=[row, kv, kv, _full(mask.shape)],
        out_specs=row, out_shape=jax.ShapeDtypeStruct((R, W), F32),
        compiler_params=_params(("parallel",)), name="mem_sample",
    )(mq, kf, vf, mask)


def _mixer_out_kernel(x_ref, g_ref, fox_ref, ret_ref, rg_ref, mem_ref, wg_ref, wf_ref, wr_ref, wm_ref, wo_ref, o_ref,
                      *, fox_transposed):
    D = x_ref.shape[1]
    x = x_ref[...]
    gates = jnp.dot(_rms(x, g_ref[...]).astype(BF16), wg_ref[...], preferred_element_type=F32)
    if fox_transposed:
        u_fox = lax.dot_general(fox_ref[...], wf_ref[...].reshape(-1, D), TN, preferred_element_type=F32)
    else:
        u_fox = jnp.dot(fox_ref[0].astype(BF16), wf_ref[0], preferred_element_type=F32)
        for h in range(1, FOX_HEADS):
            u_fox += jnp.dot(fox_ref[h].astype(BF16), wf_ref[h], preferred_element_type=F32)
    ret, rg = ret_ref[...], rg_ref[...]
    ys = []
    for h in range(RET_HEADS):
        sl = slice(h * RET_DIM, (h + 1) * RET_DIM)
        rf = ret[:, sl]
        rn = rf * lax.rsqrt(jnp.mean(rf * rf, axis=1, keepdims=True) + EPS)
        g = rg[:, sl]
        ys.append((rn * (g * _sigmoid(g))).astype(BF16))
    u_ret = jnp.dot(jnp.concatenate(ys, axis=1), wr_ref[...], preferred_element_type=F32)
    u_mem = jnp.dot(mem_ref[...].astype(BF16), wm_ref[...], preferred_element_type=F32)
    merged = (_sigmoid(gates[:, :D]) * u_fox + _sigmoid(gates[:, D:2 * D]) * u_ret
              + _sigmoid(gates[:, 2 * D:]) * u_mem)
    o_ref[...] = x + jnp.dot(merged.astype(BF16), wo_ref[...], preferred_element_type=F32)


def _mixer_out(x, g, fox_o, ret_o, rg, mem_o, w_gate, w_up_fox, w_up_ret, w_up_mem, w_out, tm):
    R, D = x.shape
    row = lambda n: pl.BlockSpec((tm, n), lambda i: (i, 0))
    fox_transposed = fox_o.ndim == 2
    if fox_transposed:
        fox_spec = pl.BlockSpec((fox_o.shape[0], tm), lambda i: (0, i))
    else:
        fox_spec = pl.BlockSpec((FOX_HEADS, tm, FOX_DIM), lambda i: (0, i, 0))
    return pl.pallas_call(
        functools.partial(_mixer_out_kernel, fox_transposed=fox_transposed), grid=(R // tm,),
        in_specs=[row(D), _full((1, D)), fox_spec,
                  row(ret_o.shape[1]), row(rg.shape[1]), row(mem_o.shape[1]), _full(w_gate.shape),
                  _full(w_up_fox.shape), _full(w_up_ret.shape), _full(w_up_mem.shape), _full(w_out.shape)],
        out_specs=row(D), out_shape=jax.ShapeDtypeStruct((R, D), F32),
        compiler_params=_params(("parallel",)), name="mixer_out",
    )(x, g, fox_o, ret_o, rg, mem_o, w_gate, w_up_fox, w_up_ret, w_up_mem, w_out)


def _router_kernel(h_ref, g_ref, wr_ref, br_ref, tri_ref, hn_ref, gate_ref, rank_ref, cnt_ref):
    hn = _rms(h_ref[...], g_ref[...])
    hn_ref[...] = hn.astype(BF16)
    logits = lax.dot_general(wr_ref[...], hn, NT, preferred_element_type=F32, precision=HI) + br_ref[...]
    E, TS = logits.shape
    e_iota = lax.broadcasted_iota(I32, (E, TS), 0)
    work = logits
    vals, hots = [], []
    for _ in range(TOP_K):
        mx = jnp.max(work, axis=0, keepdims=True)
        idx = jnp.min(jnp.where(work == mx, e_iota, E), axis=0, keepdims=True)
        hot = e_iota == idx
        vals.append(mx)
        hots.append(hot)
        work = jnp.where(hot, -jnp.inf, work)
    es = [jnp.exp(v - vals[0]) for v in vals]
    den = es[0]
    for e in es[1:]:
        den = den + e
    gate = jnp.zeros((E, TS), F32)
    sel = jnp.zeros((E, TS), F32)
    for hot, e in zip(hots, es):
        gate = jnp.where(hot, e / den, gate)
        sel = jnp.where(hot, 1.0, sel)
    cum = jnp.dot(sel.astype(BF16), tri_ref[...], preferred_element_type=F32)
    gate_ref[...] = gate
    rank_ref[...] = jnp.where(sel > 0.0, cum - 1.0, -1.0).astype(I32)
    cnt_ref[...] = jnp.broadcast_to(cum[:, TS - 1:TS], cnt_ref.shape).astype(I32)


def _router(h, g, w_router_t, b_router, ts):
    T, D = h.shape
    E = w_router_t.shape[0]
    ns = T // ts
    tri = jnp.triu(jnp.ones((ts, ts), BF16))
    tile = lambda dt: jax.ShapeDtypeStruct((ns, E, ts), dt)
    tspec = pl.BlockSpec((None, E, ts), lambda i: (i, 0, 0))
    return pl.pallas_call(
        _router_kernel, grid=(ns,),
        in_specs=[pl.BlockSpec((ts, D), lambda i: (i, 0)), _full((1, D)), _full((E, D)), _full((E, 1)),
                  _full((ts, ts))],
        out_specs=[pl.BlockSpec((ts, D), lambda i: (i, 0)), tspec, tspec,
                   pl.BlockSpec((None, E, 128), lambda i: (i, 0, 0))],
        out_shape=[jax.ShapeDtypeStruct((T, D), BF16), tile(F32), tile(I32),
                   jax.ShapeDtypeStruct((ns, E, 128), I32)],
        compiler_params=_params(("parallel",)), name="router",
    )(h, g, w_router_t, b_router, tri)


def _moe_kernel(cnt_ref, x_ref, gate_ref, rank_ref, w1_ref, b1_ref, w2_ref, b2_ref, o_ref, yw_ref, *, ns, ts, rb, cb):
    g, e, si = pl.program_id(0), pl.program_id(1), pl.program_id(2)

    @pl.when((e == 0) & (si == 0))
    def _():
        o_ref[...] = jnp.zeros_like(o_ref)
        yw_ref[...] = jnp.zeros_like(yw_ref)

    E = pl.num_programs(1)
    cnt = cnt_ref[(g * ns + si) * E + e]
    F = w2_ref.shape[0]
    t0 = pl.multiple_of(si * ts, ts)

    def block(bi, carry):
        rank = rank_ref[...]
        r0 = pl.multiple_of(bi * rb, rb)
        hit = r0 + lax.broadcasted_iota(I32, (rb, ts), 0) == rank
        onehot = jnp.where(hit, 1.0, 0.0).astype(BF16)
        xb = jnp.dot(onehot, x_ref[pl.ds(t0, ts), :], preferred_element_type=F32).astype(BF16)
        hdn = jnp.dot(xb, w1_ref[...], preferred_element_type=F32) + b1_ref[...]
        glu = jnp.minimum(hdn[:, :F], SWIGLU_LIMIT)
        lin = jnp.clip(hdn[:, F:], -SWIGLU_LIMIT, SWIGLU_LIMIT)
        act = (glu * _sigmoid(SWIGLU_ALPHA * glu) * (lin + 1.0)).astype(BF16)
        y = jnp.dot(act, w2_ref[...], preferred_element_type=F32) + b2_ref[...]
        wrow = jnp.sum(jnp.where(hit, gate_ref[...], 0.0), axis=1, keepdims=True)
        yw_ref[pl.ds(r0, rb), :] = (y * wrow).astype(BF16)
        return carry

    lax.fori_loop(0, (cnt + rb - 1) // rb, block, 0)

    def chunk(ci, carry):
        r0 = pl.multiple_of(ci * cb, cb)
        hit = r0 + lax.broadcasted_iota(I32, (cb, ts), 0) == rank_ref[...]
        onehot = jnp.where(hit, 1.0, 0.0).astype(BF16)
        o_ref[pl.ds(t0, ts), :] += lax.dot_general(onehot, yw_ref[pl.ds(r0, cb), :], TN,
                                                   preferred_element_type=F32)
        return carry

    lax.fori_loop(0, (cnt + cb - 1) // cb, chunk, 0)


def _moe(hn, gate, rank, counts, w1, b1, w2, b2, ts, ns, rb, cb):
    T, D = hn.shape
    E, _, F2 = w1.shape
    ngroups = T // (ts * ns)
    tile = lambda: pl.BlockSpec((None, None, 1, ts), lambda g, e, s, c: (g * ns + s, e, 0, 0))
    grid_spec = pltpu.PrefetchScalarGridSpec(
        num_scalar_prefetch=1, grid=(ngroups, E, ns),
        in_specs=[pl.BlockSpec((ns * ts, D), lambda g, e, s, c: (g, 0)), tile(), tile(),
                  pl.BlockSpec((None, D, F2), lambda g, e, s, c: (e, 0, 0)),
                  pl.BlockSpec((None, 1, F2), lambda g, e, s, c: (e, 0, 0)),
                  pl.BlockSpec((None, F2 // 2, D), lambda g, e, s, c: (e, 0, 0)),
                  pl.BlockSpec((None, 1, D), lambda g, e, s, c: (e, 0, 0))],
        out_specs=pl.BlockSpec((ns * ts, D), lambda g, e, s, c: (g, 0)),
        scratch_shapes=[pltpu.VMEM((pl.cdiv(pl.cdiv(ts, rb) * rb, cb) * cb, D), BF16)])
    nst = T // ts
    return pl.pallas_call(
        functools.partial(_moe_kernel, ns=ns, ts=ts, rb=rb, cb=cb), grid_spec=grid_spec,
        out_shape=jax.ShapeDtypeStruct((T, D), F32),
        compiler_params=_params(("parallel", "arbitrary", "arbitrary")), name="moe",
    )(counts, hn, gate.reshape(nst, E, 1, ts), rank.reshape(nst, E, 1, ts), w1, b1, w2, b2)


def _final_kernel(h_ref, y_ref, g_ref, o_ref):
    o_ref[...] = _rms(h_ref[...] + y_ref[...], g_ref[...])


def _final_norm(h, y, g, tm):
    R, D = h.shape
    row = pl.BlockSpec((tm, D), lambda i: (i, 0))
    return pl.pallas_call(
        _final_kernel, grid=(R // tm,), in_specs=[row, row, _full((1, D))], out_specs=row,
        out_shape=jax.ShapeDtypeStruct((R, D), F32),
        compiler_params=_params(("parallel",)), name="final_norm",
    )(h, y, g)


def _rope_tables(pos):
    half = RET_DIM // 2
    inv_freq = 1.0 / (ROPE_BASE ** jnp.linspace(0.0, 1.0, half, dtype=F32))
    ang = pos.astype(F32)[:, None] * inv_freq[None, :]
    cos, sin = jnp.cos(ang), jnp.sin(ang)
    return jnp.concatenate([cos, cos], axis=1), jnp.concatenate([-sin, sin], axis=1)


def _tile_of(n, pref):
    t = min(n, pref)
    assert n % t == 0, (n, t)
    return t


MOE_ROW_BLOCK = 160
MOE_SCATTER_BLOCK = 256


def _moe_layer(h, lw, ts, ns):
    g_moe, w_router_t, b_router, w1, b1, w2, b2 = lw
    hn, gate, rank, cnt = _router(h, g_moe, w_router_t, b_router, ts)
    counts = cnt[:, :, 0].reshape(-1)
    return _moe(hn, gate, rank, counts, w1, b1, w2, b2, ts, ns, MOE_ROW_BLOCK, MOE_SCATTER_BLOCK)


def kernel(x_prompt, x_sample, mem_prompt, cache_fox_k, cache_fox_v, cache_fox_logf, state_ret, cache_mem_k, cache_mem_v, page_table, g_mix, w_in, b_forget, w_up_fox, w_up_ret, w_up_mem, w_out, g_mem, w_mem_kv, g_moe, w_router, b_router, w_ff1, b_ff1, w_ff2, b_ff2, g_final):
    B, S, D = x_prompt.shape
    DB, T, _ = x_sample.shape
    depth = g_mix.shape[0]
    assert B == 1 and depth == 1
    NP = page_table.shape[1]
    page = cache_fox_k.shape[2]
    past_len = NP * page
    fw, rw, mw = FOX_HEADS * FOX_DIM, RET_HEADS * RET_DIM, MEM_HEADS * MEM_DIM
    R = DB * T
    l = 0

    wi = w_in[l]
    c0 = 3 * fw
    c1 = c0 + FOX_HEADS
    c2 = c1 + 4 * rw
    c3 = c2 + mw
    w_ffp = jnp.zeros((D, 128), F32).at[:, :FOX_HEADS].set(wi[:, c0:c1])
    wparts = tuple(t.astype(BF16) for t in (wi[:, :c0], w_ffp, wi[:, c1:c2], wi[:, c2:c3], wi[:, c3:]))
    bfp = jnp.zeros((1, 128), F32).at[0, :FOX_HEADS].set(b_forget[l])
    gm = g_mix[l].reshape(1, D)
    wuf = w_up_fox[l].astype(BF16).reshape(FOX_HEADS, FOX_DIM, D)
    wur, wum, wo = (t[l].astype(BF16) for t in (w_up_ret, w_up_mem, w_out))
    moe_w = (g_moe[l].reshape(1, D), jnp.swapaxes(w_router[l], 0, 1), b_router[l].reshape(-1, 1),
             w_ff1[l].astype(BF16), b_ff1[l][:, None, :], w_ff2[l].astype(BF16), b_ff2[l][:, None, :])
    gf = g_final.reshape(1, D)

    xp = x_prompt.reshape(S, D)
    tmp = _tile_of(S, 512)
    cos_p, sin_p = _rope_tables(jnp.arange(S))
    mk, mv = _memory_kv(mem_prompt[0], g_mem[l].reshape(1, D), w_mem_kv[l].astype(BF16))
    fq, fk, fv, lf, rq, rk, rv, rg, mq = _input_projection(xp, gm, wparts, bfp, cos_p, sin_p, tmp)
    qa, ka, vh = _fox_prep(fq, fk, fv, lf, tmp)
    fox_o = _fox_prompt(qa, ka, vh, _tile_of(S, 512))
    ret_o, rs_p = _ret_prompt(rq, rk, rv)
    mem_o = _mem_prompt(mq, mk, mv, tmp)
    hp = _mixer_out(xp, gm, fox_o, ret_o, rg, mem_o, wparts[4], wuf, wur, wum, wo, _tile_of(S, 256))
    ts = _tile_of(S, 1024)
    yp = _moe_layer(hp, moe_w, ts, _tile_of(S // ts, 2))
    y_prompt = _final_norm(hp, yp, gf, tmp)

    xs = x_sample.reshape(R, D)
    tms = _tile_of(R, 512)
    cos_s, sin_s = _rope_tables(past_len + jnp.arange(T))
    cos_s, sin_s = jnp.tile(cos_s, (DB, 1)), jnp.tile(sin_s, (DB, 1))
    fq_s, fk_s, fv_s, lf_s, rq_s, rk_s, rv_s, rg_s, mq_s = _input_projection(
        xs, gm, wparts, bfp, cos_s, sin_s, tms)
    fox_os = _fox_sample(fq_s, fk_s, fv_s, lf_s, page_table, cache_fox_k[l], cache_fox_v[l], cache_fox_logf[l],
                         _tile_of(NP, 16))
    ret_os, rs_s = _ret_sample(rq_s, rk_s, rv_s, state_ret[l])
    mem_os = _mem_sample(mq_s, cache_mem_k[l], cache_mem_v[l])
    hs = _mixer_out(xs, gm, fox_os, ret_os, rg_s, mem_os, wparts[4], wuf, wur, wum, wo, _tile_of(R, 256))
    tss = _tile_of(R, 1024)
    ys = _moe_layer(hs, moe_w, tss, 1)
    y_sample = _final_norm(hs, ys, gf, tms)

    return (y_prompt.reshape(B, S, D), y_sample.reshape(DB, T, D),
            fk.reshape(1, B, S, FOX_HEADS, FOX_DIM), fv.reshape(1, B, S, FOX_HEADS, FOX_DIM),
            lf.reshape(1, B, S, FOX_HEADS), rs_p.reshape(1, B, RET_HEADS, RET_DIM, RET_DIM),
            mk.reshape(1, B, -1, MEM_HEADS, MEM_DIM), mv.reshape(1, B, -1, MEM_HEADS, MEM_DIM),
            fk_s.reshape(1, DB, T, FOX_HEADS, FOX_DIM), fv_s.reshape(1, DB, T, FOX_HEADS, FOX_DIM),
            lf_s.reshape(1, DB, T, FOX_HEADS), rs_s.reshape(1, DB, RET_HEADS, RET_DIM, RET_DIM))
```

```python
import functools
import math

import jax
import jax.numpy as jnp
import numpy as np
from jax import lax
from jax.experimental import pallas as pl
from jax.experimental.pallas import tpu as pltpu

F32 = jnp.float32
BF16 = jnp.bfloat16
I32 = jnp.int32

FOX_HEADS = 8
FOX_DIM = 64
RET_HEADS = 4
RET_DIM = 128
RET_CHUNK = 128
MEM_HEADS = 4
MEM_DIM = 128
N_BRANCH = 3
TOP_K = 4
ROPE_BASE = 10000.0
SWIGLU_LIMIT = 7.0
SWIGLU_ALPHA = 1.702
EPS = 1e-6

LOG2E = 1.4426950408889634
NEG = -1e30
VMEM_LIMIT = 48 * 1024 * 1024

HI = lax.Precision.HIGHEST
NT = (((1,), (1,)), ((), ()))
TN = (((0,), (0,)), ((), ()))


def _params(sem, vmem=VMEM_LIMIT):
    return pltpu.CompilerParams(dimension_semantics=sem, vmem_limit_bytes=vmem)


def _rms(x, g):
    return x * lax.rsqrt(jnp.mean(x * x, axis=-1, keepdims=True) + EPS) * g


def _log_sigmoid(x):
    return jnp.minimum(x, 0.0) - jnp.log(1.0 + jnp.exp(-jnp.abs(x)))


def _sigmoid(x):
    return 1.0 / (1.0 + jnp.exp(-x))


def _split3(x):
    hi = x.astype(BF16)
    r = x - hi.astype(F32)
    mid = r.astype(BF16)
    lo = (r - mid.astype(F32)).astype(BF16)
    return hi, mid, lo


def _full(shape):
    nd = len(shape)
    return pl.BlockSpec(shape, lambda *_: (0,) * nd)


def _proj_fox_kernel(x_ref, g_ref, w_ref, wf_ref, bf_ref, fq_ref, fk_ref, fv_ref, lf_ref):
    xn = _rms(x_ref[...], g_ref[...]).astype(BF16)
    y = jnp.dot(xn, w_ref[...], preferred_element_type=F32)
    w = fq_ref.shape[1]
    fq_ref[...] = y[:, :w]
    fk_ref[...] = y[:, w:2 * w]
    fv_ref[...] = y[:, 2 * w:]
    ff = jnp.dot(xn, wf_ref[...], preferred_element_type=F32)
    lf_ref[...] = _log_sigmoid(ff + bf_ref[...])[:, :lf_ref.shape[1]]


def _rotate(y, cos, sin):
    outs = []
    for h in range(RET_HEADS):
        yh = y[:, h * RET_DIM:(h + 1) * RET_DIM]
        outs.append(yh * cos + pltpu.roll(yh, RET_DIM // 2, axis=1) * sin)
    return jnp.concatenate(outs, axis=1)


def _proj_ret_kernel(x_ref, g_ref, w_ref, wm_ref, cos_ref, sin_ref, rq_ref, rk_ref, rv_ref, rg_ref, mq_ref):
    xn = _rms(x_ref[...], g_ref[...]).astype(BF16)
    y = jnp.dot(xn, w_ref[...], preferred_element_type=F32)
    w = rq_ref.shape[1]
    cos, sin = cos_ref[...], sin_ref[...]
    rq_ref[...] = _rotate(y[:, :w], cos, sin)
    rk_ref[...] = _rotate(y[:, w:2 * w], cos, sin) * (RET_DIM ** -0.5)
    rv_ref[...] = y[:, 2 * w:3 * w]
    rg_ref[...] = y[:, 3 * w:]
    mq_ref[...] = jnp.dot(xn, wm_ref[...], preferred_element_type=F32)


def _input_projection(x, g, wparts, b_forget_pad, cos, sin, tm):
    R, D = x.shape
    w_fox, w_ff, w_ret, w_mem, _ = wparts
    fw = FOX_HEADS * FOX_DIM
    rw = RET_HEADS * RET_DIM
    mw = MEM_HEADS * MEM_DIM
    row = lambda n: pl.BlockSpec((tm, n), lambda i: (i, 0))
    sds = lambda n: jax.ShapeDtypeStruct((R, n), F32)
    grid = (R // tm,)
    fq, fk, fv, lf = pl.pallas_call(
        _proj_fox_kernel, grid=grid,
        in_specs=[row(D), _full((1, D)), _full(w_fox.shape), _full(w_ff.shape), _full((1, 128))],
        out_specs=[row(fw), row(fw), row(fw), row(FOX_HEADS)],
        out_shape=[sds(fw), sds(fw), sds(fw), sds(FOX_HEADS)],
        compiler_params=_params(("parallel",)), name="proj_fox",
    )(x, g, w_fox, w_ff, b_forget_pad)
    rq, rk, rv, rg, mq = pl.pallas_call(
        _proj_ret_kernel, grid=grid,
        in_specs=[row(D), _full((1, D)), _full(w_ret.shape), _full(w_mem.shape), row(RET_DIM), row(RET_DIM)],
        out_specs=[row(rw), row(rw), row(rw), row(rw), row(mw)],
        out_shape=[sds(rw), sds(rw), sds(rw), sds(rw), sds(mw)],
        compiler_params=_params(("parallel",)), name="proj_ret",
    )(x, g, w_ret, w_mem, cos, sin)
    return fq, fk, fv, lf, rq, rk, rv, rg, mq


def _memkv_kernel(x_ref, g_ref, w_ref, mk_ref, mv_ref):
    xn = _rms(x_ref[...], g_ref[...]).astype(BF16)
    y = jnp.dot(xn, w_ref[...], preferred_element_type=F32)
    w = mk_ref.shape[1]
    mk_ref[...] = y[:, :w]
    mv_ref[...] = y[:, w:]


def _memory_kv(mem, g, w):
    M, D = mem.shape
    mw = MEM_HEADS * MEM_DIM
    return pl.pallas_call(
        _memkv_kernel, grid=(1,),
        in_specs=[_full((M, D)), _full((1, D)), _full(w.shape)],
        out_specs=[_full((M, mw)), _full((M, mw))],
        out_shape=[jax.ShapeDtypeStruct((M, mw), F32)] * 2,
        compiler_params=_params(("arbitrary",)), name="memory_kv",
    )(mem, g, w)


def _fox_prep_kernel(fq_ref, fk_ref, fv_ref, lf_ref, tri_ref, qa_ref, ka_ref, vt_ref, carry_ref):
    @pl.when(pl.program_id(0) == 0)
    def _():
        carry_ref[...] = jnp.zeros_like(carry_ref)

    tm = fq_ref.shape[0]
    c = jnp.dot(tri_ref[...], lf_ref[...], preferred_element_type=F32, precision=HI) + carry_ref[...]
    carry_ref[...] = c[tm - 1:tm, :]
    hi, mid, lo = (p.astype(F32) for p in _split3(c * LOG2E))
    lane = lax.broadcasted_iota(I32, (tm, 2 * FOX_DIM), 1)
    d = FOX_DIM
    fq, fk, fv = fq_ref[...], fk_ref[...], fv_ref[...]
    for h in range(FOX_HEADS):
        g = h // 2
        sl = slice(g * 2 * d, (g + 1) * 2 * d)
        qg, kg = fq[:, sl], fk[:, sl]
        if h % 2:
            qg, kg = (pltpu.roll(t, d, axis=1) for t in (qg, kg))
        ch, cm, cl = hi[:, h:h + 1], mid[:, h:h + 1], lo[:, h:h + 1]
        one = jnp.where((lane >= d + 3) & (lane < d + 6), 1.0, 0.0)
        qa = jnp.where(lane < d, qg * (d ** -0.5 * LOG2E),
                       jnp.where(lane == d, ch, jnp.where(lane == d + 1, cm, jnp.where(lane == d + 2, cl, one))))
        onek = jnp.where((lane >= d) & (lane < d + 3), 1.0, 0.0)
        ka = jnp.where(lane < d, kg,
                       jnp.where(lane == d + 3, -ch, jnp.where(lane == d + 4, -cm, jnp.where(lane == d + 5, -cl, onek))))
        qa_ref[h] = qa.astype(BF16)
        ka_ref[h] = ka.astype(BF16)
    vt_ref[...] = fv.T.astype(BF16)


def _fox_prep(fq, fk, fv, lf, tm):
    S, W = fq.shape
    tri = jnp.tril(jnp.ones((tm, tm), F32))
    row = lambda n: pl.BlockSpec((tm, n), lambda i: (i, 0))
    hm = lambda n: pl.BlockSpec((FOX_HEADS, tm, n), lambda i: (0, i, 0))
    return pl.pallas_call(
        _fox_prep_kernel, grid=(S // tm,),
        in_specs=[row(W), row(W), row(W), row(FOX_HEADS), _full((tm, tm))],
        out_specs=[hm(2 * FOX_DIM), hm(2 * FOX_DIM), pl.BlockSpec((W, tm), lambda i: (0, i))],
        out_shape=[jax.ShapeDtypeStruct((FOX_HEADS, S, 2 * FOX_DIM), BF16)] * 2
        + [jax.ShapeDtypeStruct((W, S), BF16)],
        scratch_shapes=[pltpu.VMEM((1, FOX_HEADS), F32)],
        compiler_params=_params(("arbitrary",)), name="fox_prep",
    )(fq, fk, fv, lf, tri)


def _fox_prompt_kernel(q_ref, k_ref, vt_ref, o_ref, s0_ref, s1_ref, m_ref, l_ref, acc_ref, *, t):
    i = pl.program_id(1)
    q = q_ref[...]

    def scores(j, dst):
        k0 = pl.multiple_of(j * t, t)
        dst[...] = lax.dot_general(k_ref[pl.ds(k0, t), :], q, NT, preferred_element_type=F32)

    def update(j, src, masked):
        k0 = pl.multiple_of(j * t, t)
        st = src[...]
        if masked:
            st = jnp.where(lax.broadcasted_iota(I32, (t, t), 0) <= lax.broadcasted_iota(I32, (t, t), 1), st, NEG)
        m = m_ref[...]
        m_new = jnp.maximum(m, jnp.max(st, axis=0, keepdims=True))
        alpha = jnp.exp2(m - m_new)
        p = jnp.exp2(st - m_new)
        l_ref[...] = alpha * l_ref[...] + jnp.sum(p, axis=0, keepdims=True)
        acc_ref[...] = alpha * acc_ref[...] + jnp.dot(vt_ref[:, pl.ds(k0, t)], p.astype(BF16),
                                                      preferred_element_type=F32)
        m_ref[...] = m_new

    m_ref[...] = jnp.full_like(m_ref, NEG)
    l_ref[...] = jnp.zeros_like(l_ref)
    acc_ref[...] = jnp.zeros_like(acc_ref)
    scores(0, s0_ref)

    def body(jp, carry):
        scores(2 * jp + 1, s1_ref)
        update(2 * jp, s0_ref, False)
        scores(2 * jp + 2, s0_ref)
        update(2 * jp + 1, s1_ref, False)
        return carry

    lax.fori_loop(0, i // 2, body, 0)

    @pl.when(i % 2 == 0)
    def _():
        update(i, s0_ref, True)

    @pl.when(i % 2 == 1)
    def _():
        scores(i, s1_ref)
        update(i - 1, s0_ref, False)
        update(i, s1_ref, True)

    o_ref[...] = (acc_ref[...] / l_ref[...]).astype(o_ref.dtype)


def _fox_prompt(qa, ka, vt, t):
    H, S, A = qa.shape
    return pl.pallas_call(
        functools.partial(_fox_prompt_kernel, t=t), grid=(H, S // t),
        in_specs=[pl.BlockSpec((None, t, A), lambda h, i: (h, i, 0)),
                  pl.BlockSpec((None, S, A), lambda h, i: (h, 0, 0)),
                  pl.BlockSpec((FOX_DIM, S), lambda h, i: (h, 0))],
        out_specs=pl.BlockSpec((FOX_DIM, t), lambda h, i: (h, i)),
        out_shape=jax.ShapeDtypeStruct((H * FOX_DIM, S), BF16),
        scratch_shapes=[pltpu.VMEM((t, t), F32), pltpu.VMEM((t, t), F32), pltpu.VMEM((1, t), F32),
                        pltpu.VMEM((1, t), F32), pltpu.VMEM((FOX_DIM, t), F32)],
        compiler_params=_params(("parallel", "arbitrary")), name="fox_prompt",
    )(qa, ka, vt)


def _fox_sample_kernel(pt_ref, q_ref, kn_ref, vn_ref, lfn_ref, lfnt_ref, ms_ref, *rest, pp):
    k_refs, v_refs, l_refs = rest[:pp], rest[pp:2 * pp], rest[2 * pp:3 * pp]
    o_ref = rest[3 * pp]
    m_ref, l_ref, acc_ref, carry_ref, qbd_ref, rowc_ref = rest[3 * pp + 1:]
    j = pl.program_id(1)
    H, d, T = FOX_HEADS, FOX_DIM, q_ref.shape[0]
    HT, W = H * T, H * d
    page = ms_ref.shape[0]

    @pl.when(j == 0)
    def _():
        qsc = q_ref[...] * (d ** -0.5 * LOG2E)
        lfn, lfnt = lfn_ref[...], lfnt_ref[...]
        ti = lax.broadcasted_iota(I32, (T, T), 0)
        tj = lax.broadcasted_iota(I32, (T, T), 1)
        low = jnp.where(tj <= ti, 1.0, 0.0)
        up = jnp.where(ti <= tj, 1.0, 0.0)
        cq = jnp.zeros((T, H), F32)
        cqt = jnp.zeros((H, T), F32)
        for t in range(T):
            cq = cq + low[:, t:t + 1] * lfn[t:t + 1, :]
            cqt = cqt + lfnt[:, t:t + 1] * up[t:t + 1, :]
        bias = jnp.concatenate([cq[:, h:h + 1] - cqt[h:h + 1, :] for h in range(H)], axis=0) * LOG2E
        rowc_ref[...] = jnp.concatenate([cq[:, h:h + 1] for h in range(H)], axis=0) * LOG2E
        r_i = lax.broadcasted_iota(I32, (HT, W), 0)
        c_i = lax.broadcasted_iota(I32, (HT, W), 1)
        qbd = jnp.where(c_i // d == r_i // T, jnp.concatenate([qsc] * H, axis=0), 0.0).astype(BF16)
        qbd_ref[...] = qbd
        s = lax.dot_general(qbd, kn_ref[...].astype(BF16), NT, preferred_element_type=F32) + bias
        rr = lax.broadcasted_iota(I32, (HT, T), 0)
        cc = lax.broadcasted_iota(I32, (HT, T), 1)
        s = jnp.where(cc <= rr % T, s, NEG)
        m0 = jnp.max(s, axis=1, keepdims=True)
        p = jnp.exp2(s - m0)
        m_ref[...] = m0
        l_ref[...] = jnp.sum(p, axis=1, keepdims=True)
        acc_ref[...] = jnp.dot(p.astype(BF16), vn_ref[...].astype(BF16), preferred_element_type=F32)
        carry_ref[...] = jnp.zeros_like(carry_ref)

    qbd = qbd_ref[...]
    rowc = rowc_ref[...]
    ms = ms_ref[...]
    lf = jnp.concatenate([l_refs[r][...] for r in range(pp)], axis=0)
    pieces = jnp.concatenate([pc.astype(F32) for pc in _split3(lf)], axis=0).astype(BF16)
    rr3 = jnp.dot(pieces, ms, preferred_element_type=F32)
    n = pp * H
    rr = rr3[:n] + rr3[n:2 * n] + rr3[2 * n:]
    carry = carry_ref[...]
    biases = []
    for r in range(pp):
        rs = (rr[r * H:(r + 1) * H, :page] + carry[:, 0:1]) * LOG2E
        carry = carry + rr[r * H:(r + 1) * H, page:]
        biases.append(jnp.concatenate([jnp.broadcast_to(rs[h:h + 1, :], (T, page)) for h in range(H)], axis=0))
    carry_ref[...] = carry
    bias = jnp.concatenate(biases, axis=1) + rowc
    kb = jnp.concatenate([k_refs[r][...] for r in range(pp)], axis=1).astype(BF16)
    vb = jnp.concatenate([v_refs[r][...] for r in range(pp)], axis=1).astype(BF16)
    s = jnp.dot(qbd, kb, preferred_element_type=F32) + bias
    m = m_ref[...]
    m_new = jnp.maximum(m, jnp.max(s, axis=1, keepdims=True))
    alpha = jnp.exp2(m - m_new)
    p = jnp.exp2(s - m_new)
    l_ref[...] = alpha * l_ref[...] + jnp.sum(p, axis=1, keepdims=True)
    acc_ref[...] = alpha * acc_ref[...] + lax.dot_general(p.astype(BF16), vb, NT, preferred_element_type=F32)
    m_ref[...] = m_new

    @pl.when(j == pl.num_programs(1) - 1)
    def _():
        acc = acc_ref[...]
        o = jnp.concatenate([acc[h * T:(h + 1) * T, h * d:(h + 1) * d] for h in range(H)], axis=0)
        o_ref[...] = (o / l_ref[...]).reshape(H, T, d)


def _fox_sample(q, kn, vn, lfn, page_table, cache_k, cache_v, cache_lf, pp):
    R, W = q.shape
    DB, NP = page_table.shape
    T = R // DB
    n_pool, page, H, d = cache_k.shape
    ck = jnp.transpose(cache_k, (0, 2, 3, 1)).reshape(n_pool, H * d, page)
    cv = jnp.transpose(cache_v, (0, 2, 3, 1)).reshape(n_pool, H * d, page)
    cl = jnp.transpose(cache_lf, (0, 2, 1))
    lfnt = jnp.swapaxes(lfn.reshape(DB, T, H), 1, 2)
    tt = np.arange(page)
    ms = np.concatenate([tt[:, None] > tt[None, :], np.ones((page, 128), bool)], axis=1)
    ms = jnp.asarray(ms, BF16)
    pt = page_table.reshape(-1).astype(I32)

    def page_spec(shape, r):
        return pl.BlockSpec((None,) + shape, lambda b, j, pt: (pt[b * NP + NP - 1 - (j * pp + r)], 0, 0))

    tok = lambda n: pl.BlockSpec((T, n), lambda b, j, pt: (b, 0))
    in_specs = [tok(W), tok(W), tok(W), tok(H), pl.BlockSpec((None, H, T), lambda b, j, pt: (b, 0, 0)),
                pl.BlockSpec(ms.shape, lambda b, j, pt: (0, 0))]
    in_specs += [page_spec((H * d, page), r) for r in range(pp)]
    in_specs += [page_spec((H * d, page), r) for r in range(pp)]
    in_specs += [page_spec((H, page), r) for r in range(pp)]
    grid_spec = pltpu.PrefetchScalarGridSpec(
        num_scalar_prefetch=1, grid=(DB, NP // pp), in_specs=in_specs,
        out_specs=pl.BlockSpec((H, T, d), lambda b, j, pt: (0, b, 0)),
        scratch_shapes=[pltpu.VMEM((H * T, 1), F32), pltpu.VMEM((H * T, 1), F32), pltpu.VMEM((H * T, W), F32),
                        pltpu.VMEM((H, 128), F32), pltpu.VMEM((H * T, W), BF16), pltpu.VMEM((H * T, 1), F32)])
    return pl.pallas_call(
        functools.partial(_fox_sample_kernel, pp=pp), grid_spec=grid_spec,
        out_shape=jax.ShapeDtypeStruct((H, R, d), F32),
        compiler_params=_params(("parallel", "arbitrary")), name="fox_sample",
    )(pt, q, kn, vn, lfn, lfnt, ms, *([ck] * pp), *([cv] * pp), *([cl] * pp))


def _ret_consts(C):
    lg = jnp.log(1.0 - jnp.exp2(-5.0 - jnp.arange(RET_HEADS, dtype=F32)))
    idx = jnp.arange(C, dtype=F32)
    rel = idx[:, None] - idx[None, :]
    intra = jnp.where(rel >= 0, jnp.exp(jnp.maximum(rel, 0.0)[None] * lg[:, None, None]), 0.0)
    qd = jnp.exp((idx + 1.0)[None, :] * lg[:, None])
    kd = jnp.exp((C - 1.0 - idx)[None, :] * lg[:, None])
    cd = jnp.exp(C * lg)
    rep = lambda t: jnp.broadcast_to(t[:, :, None], t.shape + (RET_DIM,))
    return intra, rep(qd), rep(kd), jnp.broadcast_to(cd[:, None, None], (RET_HEADS, 1, RET_DIM))


def _ret_heads(q, k, v, state_of, intra_ref, qd_ref, kd_ref, cd_ref):
    outs, states = [], []
    for h in range(RET_HEADS):
        sl = slice(h * RET_DIM, (h + 1) * RET_DIM)
        qh, kh, vh = q[:, sl].astype(BF16), k[:, sl], v[:, sl].astype(BF16)
        st = state_of(h)
        a = lax.dot_general(qh, kh.astype(BF16), NT, preferred_element_type=F32) * intra_ref[h]
        o = (jnp.dot(a.astype(BF16), vh, preferred_element_type=F32)
             + jnp.dot(qh, st.astype(BF16), preferred_element_type=F32) * qd_ref[h])
        kdec = (kh * kd_ref[h]).astype(BF16)
        states.append(cd_ref[h] * st + lax.dot_general(kdec, vh, TN, preferred_element_type=F32))
        outs.append(o)
    return jnp.concatenate(outs, axis=1), states


def _ret_prompt_kernel(q_ref, k_ref, v_ref, intra_ref, qd_ref, kd_ref, cd_ref, o_ref, so_ref, st_ref):
    @pl.when(pl.program_id(0) == 0)
    def _():
        st_ref[...] = jnp.zeros_like(st_ref)

    o, states = _ret_heads(q_ref[...], k_ref[...], v_ref[...], lambda h: st_ref[h],
                           intra_ref, qd_ref, kd_ref, cd_ref)
    o_ref[...] = o
    for h in range(RET_HEADS):
        st_ref[h] = states[h]

    @pl.when(pl.program_id(0) == pl.num_programs(0) - 1)
    def _():
        so_ref[...] = st_ref[...]


def _ret_prompt(rq, rk, rv):
    S, W = rq.shape
    C = min(RET_CHUNK, S)
    consts = _ret_consts(C)
    row = pl.BlockSpec((C, W), lambda i: (i, 0))
    st_shape = (RET_HEADS, RET_DIM, RET_DIM)
    return pl.pallas_call(
        _ret_prompt_kernel, grid=(S // C,),
        in_specs=[row, row, row] + [_full(c.shape) for c in consts],
        out_specs=[row, _full(st_shape)],
        out_shape=[jax.ShapeDtypeStruct((S, W), F32), jax.ShapeDtypeStruct(st_shape, F32)],
        scratch_shapes=[pltpu.VMEM(st_shape, F32)],
        compiler_params=_params(("arbitrary",)), name="ret_prompt",
    )(rq, rk, rv, *consts)


SAMPLE_SEQS_PER_STEP = 8


def _ret_sample_kernel(q_ref, k_ref, v_ref, s0_ref, intra_ref, qd_ref, kd_ref, cd_ref, o_ref, so_ref):
    nb = s0_ref.shape[0]
    T = q_ref.shape[0] // nb
    for b in range(nb):
        rows = pl.ds(b * T, T)
        o, states = _ret_heads(q_ref[rows, :], k_ref[rows, :], v_ref[rows, :], lambda h: s0_ref[b, h],
                               intra_ref, qd_ref, kd_ref, cd_ref)
        o_ref[rows, :] = o
        for h in range(RET_HEADS):
            so_ref[b, h] = states[h]


def _ret_sample(rq, rk, rv, state0):
    R, W = rq.shape
    DB = state0.shape[0]
    T = R // DB
    nb = _tile_of(DB, SAMPLE_SEQS_PER_STEP)
    consts = _ret_consts(min(RET_CHUNK, T))
    row = pl.BlockSpec((nb * T, W), lambda b: (b, 0))
    st = pl.BlockSpec((nb, RET_HEADS, RET_DIM, RET_DIM), lambda b: (b, 0, 0, 0))
    return pl.pallas_call(
        _ret_sample_kernel, grid=(DB // nb,),
        in_specs=[row, row, row, st] + [_full(c.shape) for c in consts],
        out_specs=[row, st],
        out_shape=[jax.ShapeDtypeStruct((R, W), F32), jax.ShapeDtypeStruct(state0.shape, F32)],
        compiler_params=_params(("parallel",)), name="ret_sample",
    )(rq, rk, rv, state0, *consts)


def _softmax_rows(s):
    e = jnp.exp(s - jnp.max(s, axis=1, keepdims=True))
    return e / jnp.sum(e, axis=1, keepdims=True)


def _mem_prompt_kernel(q_ref, k_ref, v_ref, o_ref):
    q, k, v = q_ref[...], k_ref[...], v_ref[...]
    outs = []
    for h in range(MEM_HEADS):
        sl = slice(h * MEM_DIM, (h + 1) * MEM_DIM)
        s = lax.dot_general(q[:, sl].astype(BF16), k[:, sl].astype(BF16), NT,
                            preferred_element_type=F32) * (MEM_DIM ** -0.5)
        p = _softmax_rows(s).astype(BF16)
        outs.append(jnp.dot(p, v[:, sl].astype(BF16), preferred_element_type=F32))
    o_ref[...] = jnp.concatenate(outs, axis=1).astype(o_ref.dtype)


def _mem_prompt(mq, mk, mv, tm):
    S, W = mq.shape
    row = pl.BlockSpec((tm, W), lambda i: (i, 0))
    return pl.pallas_call(
        _mem_prompt_kernel, grid=(S // tm,),
        in_specs=[row, _full(mk.shape), _full(mv.shape)],
        out_specs=row, out_shape=jax.ShapeDtypeStruct((S, W), BF16),
        compiler_params=_params(("parallel",)), name="mem_prompt",
    )(mq, mk, mv)


def _mem_sample_kernel(q_ref, k_ref, v_ref, mask_ref, o_ref):
    nb = k_ref.shape[0]
    T = q_ref.shape[0] // nb
    mask = mask_ref[...]
    for b in range(nb):
        q = q_ref[pl.ds(b * T, T), :]
        qm = jnp.concatenate([q[:, h * MEM_DIM:(h + 1) * MEM_DIM] for h in range(MEM_HEADS)], axis=0).astype(BF16)
        s = lax.dot_general(qm, k_ref[b].astype(BF16), NT, preferred_element_type=F32) * (MEM_DIM ** -0.5)
        p = _softmax_rows(s + mask).astype(BF16)
        o = jnp.dot(p, v_ref[b].astype(BF16), preferred_element_type=F32)
        o_ref[pl.ds(b * T, T), :] = jnp.concatenate([o[h * T:(h + 1) * T, :] for h in range(MEM_HEADS)], axis=1)


def _mem_sample(mq, cache_k, cache_v):
    R, W = mq.shape
    DB, M, H, d = cache_k.shape
    T = R // DB
    kf = cache_k.reshape(DB, M * H, d)
    vf = cache_v.reshape(DB, M * H, d)
    mask = np.where((np.arange(M * H) % H)[None, :] == (np.arange(H * T) // T)[:, None], 0.0, NEG)
    mask = jnp.asarray(mask, F32)
    nb = _tile_of(DB, SAMPLE_SEQS_PER_STEP)
    row = pl.BlockSpec((nb * T, W), lambda b: (b, 0))
    kv = pl.BlockSpec((nb, M * H, d), lambda b: (b, 0, 0))
    return pl.pallas_call(
        _mem_sample_kernel, grid=(DB // nb,),
        in_specs=[row, kv, kv, _full(mask.shape)],
        out_specs=row, out_shape=jax.ShapeDtypeStruct((R, W), F32),
        compiler_params=_params(("parallel",)), name="mem_sample",
    )(mq, kf, vf, mask)


def _mixer_out_kernel(x_ref, g_ref, fox_ref, ret_ref, rg_ref, mem_ref, wg_ref, wf_ref, wr_ref, wm_ref, wo_ref, o_ref,
                      *, fox_transposed):
    D = x_ref.shape[1]
    x = x_ref[...]
    gates = jnp.dot(_rms(x, g_ref[...]).astype(BF16), wg_ref[...], preferred_element_type=F32)
    if fox_transposed:
        u_fox = lax.dot_general(fox_ref[...], wf_ref[...].reshape(-1, D), TN, preferred_element_type=F32)
    else:
        u_fox = jnp.dot(fox_ref[0].astype(BF16), wf_ref[0], preferred_element_type=F32)
        for h in range(1, FOX_HEADS):
            u_fox += jnp.dot(fox_ref[h].astype(BF16), wf_ref[h], preferred_element_type=F32)
    ret, rg = ret_ref[...], rg_ref[...]
    ys = []
    for h in range(RET_HEADS):
        sl = slice(h * RET_DIM, (h + 1) * RET_DIM)
        rf = ret[:, sl]
        rn = rf * lax.rsqrt(jnp.mean(rf * rf, axis=1, keepdims=True) + EPS)
        g = rg[:, sl]
        ys.append((rn * (g * _sigmoid(g))).astype(BF16))
    u_ret = jnp.dot(jnp.concatenate(ys, axis=1), wr_ref[...], preferred_element_type=F32)
    u_mem = jnp.dot(mem_ref[...].astype(BF16), wm_ref[...], preferred_element_type=F32)
    merged = (_sigmoid(gates[:, :D]) * u_fox + _sigmoid(gates[:, D:2 * D]) * u_ret
              + _sigmoid(gates[:, 2 * D:]) * u_mem)
    o_ref[...] = x + jnp.dot(merged.astype(BF16), wo_ref[...], preferred_element_type=F32)


def _mixer_out(x, g, fox_o, ret_o, rg, mem_o, w_gate, w_up_fox, w_up_ret, w_up_mem, w_out, tm):
    R, D = x.shape
    row = lambda n: pl.BlockSpec((tm, n), lambda i: (i, 0))
    fox_transposed = fox_o.ndim == 2
    if fox_transposed:
        fox_spec = pl.BlockSpec((fox_o.shape[0], tm), lambda i: (0, i))
    else:
        fox_spec = pl.BlockSpec((FOX_HEADS, tm, FOX_DIM), lambda i: (0, i, 0))
    return pl.pallas_call(
        functools.partial(_mixer_out_kernel, fox_transposed=fox_transposed), grid=(R // tm,),
        in_specs=[row(D), _full((1, D)), fox_spec,
                  row(ret_o.shape[1]), row(rg.shape[1]), row(mem_o.shape[1]), _full(w_gate.shape),
                  _full(w_up_fox.shape), _full(w_up_ret.shape), _full(w_up_mem.shape), _full(w_out.shape)],
        out_specs=row(D), out_shape=jax.ShapeDtypeStruct((R, D), F32),
        compiler_params=_params(("parallel",)), name="mixer_out",
    )(x, g, fox_o, ret_o, rg, mem_o, w_gate, w_up_fox, w_up_ret, w_up_mem, w_out)


def _router_kernel(h_ref, g_ref, wr_ref, br_ref, tri_ref, hn_ref, gate_ref, rank_ref, cnt_ref):
    hn = _rms(h_ref[...], g_ref[...])
    hn_ref[...] = hn.astype(BF16)
    logits = lax.dot_general(wr_ref[...], hn, NT, preferred_element_type=F32, precision=HI) + br_ref[...]
    E, TS = logits.shape
    e_iota = lax.broadcasted_iota(I32, (E, TS), 0)
    work = logits
    vals, hots = [], []
    for _ in range(TOP_K):
        mx = jnp.max(work, axis=0, keepdims=True)
        idx = jnp.min(jnp.where(work == mx, e_iota, E), axis=0, keepdims=True)
        hot = e_iota == idx
        vals.append(mx)
        hots.append(hot)
        work = jnp.where(hot, -jnp.inf, work)
    es = [jnp.exp(v - vals[0]) for v in vals]
    den = es[0]
    for e in es[1:]:
        den = den + e
    gate = jnp.zeros((E, TS), F32)
    sel = jnp.zeros((E, TS), F32)
    for hot, e in zip(hots, es):
        gate = jnp.where(hot, e / den, gate)
        sel = jnp.where(hot, 1.0, sel)
    cum = jnp.dot(sel.astype(BF16), tri_ref[...], preferred_element_type=F32)
    gate_ref[...] = gate
    rank_ref[...] = jnp.where(sel > 0.0, cum - 1.0, -1.0).astype(I32)
    cnt_ref[...] = jnp.broadcast_to(cum[:, TS - 1:TS], cnt_ref.shape).astype(I32)


def _router(h, g, w_router_t, b_router, ts):
    T, D = h.shape
    E = w_router_t.shape[0]
    ns = T // ts
    tri = jnp.triu(jnp.ones((ts, ts), BF16))
    tile = lambda dt: jax.ShapeDtypeStruct((ns, E, ts), dt)
    tspec = pl.BlockSpec((None, E, ts), lambda i: (i, 0, 0))
    return pl.pallas_call(
        _router_kernel, grid=(ns,),
        in_specs=[pl.BlockSpec((ts, D), lambda i: (i, 0)), _full((1, D)), _full((E, D)), _full((E, 1)),
                  _full((ts, ts))],
        out_specs=[pl.BlockSpec((ts, D), lambda i: (i, 0)), tspec, tspec,
                   pl.BlockSpec((None, E, 128), lambda i: (i, 0, 0))],
        out_shape=[jax.ShapeDtypeStruct((T, D), BF16), tile(F32), tile(I32),
                   jax.ShapeDtypeStruct((ns, E, 128), I32)],
        compiler_params=_params(("parallel",)), name="router",
    )(h, g, w_router_t, b_router, tri)


def _moe_kernel(cnt_ref, x_ref, gate_ref, rank_ref, w1_ref, b1_ref, w2_ref, b2_ref, o_ref, yw_ref, *, ns, ts, rb, cb):
    g, e, si = pl.program_id(0), pl.program_id(1), pl.program_id(2)

    @pl.when((e == 0) & (si == 0))
    def _():
        o_ref[...] = jnp.zeros_like(o_ref)
        yw_ref[...] = jnp.zeros_like(yw_ref)

    E = pl.num_programs(1)
    cnt = cnt_ref[(g * ns + si) * E + e]
    F = w2_ref.shape[0]
    t0 = pl.multiple_of(si * ts, ts)

    def block(bi, carry):
        rank = rank_ref[...]
        r0 = pl.multiple_of(bi * rb, rb)
        hit = r0 + lax.broadcasted_iota(I32, (rb, ts), 0) == rank
        onehot = jnp.where(hit, 1.0, 0.0).astype(BF16)
        xb = jnp.dot(onehot, x_ref[pl.ds(t0, ts), :], preferred_element_type=F32).astype(BF16)
        hdn = jnp.dot(xb, w1_ref[...], preferred_element_type=F32) + b1_ref[...]
        glu = jnp.minimum(hdn[:, :F], SWIGLU_LIMIT)
        lin = jnp.clip(hdn[:, F:], -SWIGLU_LIMIT, SWIGLU_LIMIT)
        act = (glu * _sigmoid(SWIGLU_ALPHA * glu) * (lin + 1.0)).astype(BF16)
        y = jnp.dot(act, w2_ref[...], preferred_element_type=F32) + b2_ref[...]
        wrow = jnp.sum(jnp.where(hit, gate_ref[...], 0.0), axis=1, keepdims=True)
        yw_ref[pl.ds(r0, rb), :] = (y * wrow).astype(BF16)
        return carry

    lax.fori_loop(0, (cnt + rb - 1) // rb, block, 0)

    def chunk(ci, carry):
        r0 = pl.multiple_of(ci * cb, cb)
        hit = r0 + lax.broadcasted_iota(I32, (cb, ts), 0) == rank_ref[...]
        onehot = jnp.where(hit, 1.0, 0.0).astype(BF16)
        o_ref[pl.ds(t0, ts), :] += lax.dot_general(onehot, yw_ref[pl.ds(r0, cb), :], TN,
                                                   preferred_element_type=F32)
        return carry

    lax.fori_loop(0, (cnt + cb - 1) // cb, chunk, 0)


def _moe(hn, gate, rank, counts, w1, b1, w2, b2, ts, ns, rb, cb):
    T, D = hn.shape
    E, _, F2 = w1.shape
    ngroups = T // (ts * ns)
    tile = lambda: pl.BlockSpec((None, None, 1, ts), lambda g, e, s, c: (g * ns + s, e, 0, 0))
    grid_spec = pltpu.PrefetchScalarGridSpec(
        num_scalar_prefetch=1, grid=(ngroups, E, ns),
        in_specs=[pl.BlockSpec((ns * ts, D), lambda g, e, s, c: (g, 0)), tile(), tile(),
                  pl.BlockSpec((None, D, F2), lambda g, e, s, c: (e, 0, 0)),
                  pl.BlockSpec((None, 1, F2), lambda g, e, s, c: (e, 0, 0)),
                  pl.BlockSpec((None, F2 // 2, D), lambda g, e, s, c: (e, 0, 0)),
                  pl.BlockSpec((None, 1, D), lambda g, e, s, c: (e, 0, 0))],
        out_specs=pl.BlockSpec((ns * ts, D), lambda g, e, s, c: (g, 0)),
        scratch_shapes=[pltpu.VMEM((pl.cdiv(pl.cdiv(ts, rb) * rb, cb) * cb, D), BF16)])
    nst = T // ts
    return pl.pallas_call(
        functools.partial(_moe_kernel, ns=ns, ts=ts, rb=rb, cb=cb), grid_spec=grid_spec,
        out_shape=jax.ShapeDtypeStruct((T, D), F32),
        compiler_params=_params(("parallel", "arbitrary", "arbitrary")), name="moe",
    )(counts, hn, gate.reshape(nst, E, 1, ts), rank.reshape(nst, E, 1, ts), w1, b1, w2, b2)


def _final_kernel(h_ref, y_ref, g_ref, o_ref):
    o_ref[...] = _rms(h_ref[...] + y_ref[...], g_ref[...])


def _final_norm(h, y, g, tm):
    R, D = h.shape
    row = pl.BlockSpec((tm, D), lambda i: (i, 0))
    return pl.pallas_call(
        _final_kernel, grid=(R // tm,), in_specs=[row, row, _full((1, D))], out_specs=row,
        out_shape=jax.ShapeDtypeStruct((R, D), F32),
        compiler_params=_params(("parallel",)), name="final_norm",
    )(h, y, g)


def _rope_tables(pos):
    half = RET_DIM // 2
    inv_freq = 1.0 / (ROPE_BASE ** jnp.linspace(0.0, 1.0, half, dtype=F32))
    ang = pos.astype(F32)[:, None] * inv_freq[None, :]
    cos, sin = jnp.cos(ang), jnp.sin(ang)
    return jnp.concatenate([cos, cos], axis=1), jnp.concatenate([-sin, sin], axis=1)


def _tile_of(n, pref):
    t = min(n, pref)
    assert n % t == 0, (n, t)
    return t


MOE_ROW_BLOCK = 160
MOE_SCATTER_BLOCK = 256


def _moe_layer(h, lw, ts, ns):
    g_moe, w_router_t, b_router, w1, b1, w2, b2 = lw
    hn, gate, rank, cnt = _router(h, g_moe, w_router_t, b_router, ts)
    counts = cnt[:, :, 0].reshape(-1)
    return _moe(hn, gate, rank, counts, w1, b1, w2, b2, ts, ns, MOE_ROW_BLOCK, MOE_SCATTER_BLOCK)


def kernel(x_prompt, x_sample, mem_prompt, cache_fox_k, cache_fox_v, cache_fox_logf, state_ret, cache_mem_k, cache_mem_v, page_table, g_mix, w_in, b_forget, w_up_fox, w_up_ret, w_up_mem, w_out, g_mem, w_mem_kv, g_moe, w_router, b_router, w_ff1, b_ff1, w_ff2, b_ff2, g_final):
    B, S, D = x_prompt.shape
    DB, T, _ = x_sample.shape
    depth = g_mix.shape[0]
    assert B == 1 and depth == 1
    NP = page_table.shape[1]
    page = cache_fox_k.shape[2]
    past_len = NP * page
    fw, rw, mw = FOX_HEADS * FOX_DIM, RET_HEADS * RET_DIM, MEM_HEADS * MEM_DIM
    R = DB * T
    l = 0

    wi = w_in[l]
    c0 = 3 * fw
    c1 = c0 + FOX_HEADS
    c2 = c1 + 4 * rw
    c3 = c2 + mw
    w_ffp = jnp.zeros((D, 128), F32).at[:, :FOX_HEADS].set(wi[:, c0:c1])
    wparts = tuple(t.astype(BF16) for t in (wi[:, :c0], w_ffp, wi[:, c1:c2], wi[:, c2:c3], wi[:, c3:]))
    bfp = jnp.zeros((1, 128), F32).at[0, :FOX_HEADS].set(b_forget[l])
    gm = g_mix[l].reshape(1, D)
    wuf = w_up_fox[l].astype(BF16).reshape(FOX_HEADS, FOX_DIM, D)
    wur, wum, wo = (t[l].astype(BF16) for t in (w_up_ret, w_up_mem, w_out))
    moe_w = (g_moe[l].reshape(1, D), jnp.swapaxes(w_router[l], 0, 1), b_router[l].reshape(-1, 1),
             w_ff1[l].astype(BF16), b_ff1[l][:, None, :], w_ff2[l].astype(BF16), b_ff2[l][:, None, :])
    gf = g_final.reshape(1, D)

    xp = x_prompt.reshape(S, D)
    tmp = _tile_of(S, 512)
    cos_p, sin_p = _rope_tables(jnp.arange(S))
    mk, mv = _memory_kv(mem_prompt[0], g_mem[l].reshape(1, D), w_mem_kv[l].astype(BF16))
    fq, fk, fv, lf, rq, rk, rv, rg, mq = _input_projection(xp, gm, wparts, bfp, cos_p, sin_p, tmp)
    qa, ka, vh = _fox_prep(fq, fk, fv, lf, tmp)
    fox_o = _fox_prompt(qa, ka, vh, _tile_of(S, 512))
    ret_o, rs_p = _ret_prompt(rq, rk, rv)
    mem_o = _mem_prompt(mq, mk, mv, tmp)
    hp = _mixer_out(xp, gm, fox_o, ret_o, rg, mem_o, wparts[4], wuf, wur, wum, wo, _tile_of(S, 256))
    ts = _tile_of(S, 1024)
    yp = _moe_layer(hp, moe_w, ts, _tile_of(S // ts, 2))
    y_prompt = _final_norm(hp, yp, gf, tmp)

    xs = x_sample.reshape(R, D)
    tms = _tile_of(R, 512)
    cos_s, sin_s = _rope_tables(past_len + jnp.arange(T))
    cos_s, sin_s = jnp.tile(cos_s, (DB, 1)), jnp.tile(sin_s, (DB, 1))
    fq_s, fk_s, fv_s, lf_s, rq_s, rk_s, rv_s, rg_s, mq_s = _input_projection(
        xs, gm, wparts, bfp, cos_s, sin_s, tms)
    fox_os = _fox_sample(fq_s, fk_s, fv_s, lf_s, page_table, cache_fox_k[l], cache_fox_v[l], cache_fox_logf[l],
                         _tile_of(NP, 32))
    ret_os, rs_s = _ret_sample(rq_s, rk_s, rv_s, state_ret[l])
    mem_os = _mem_sample(mq_s, cache_mem_k[l], cache_mem_v[l])
    hs = _mixer_out(xs, gm, fox_os, ret_os, rg_s, mem_os, wparts[4], wuf, wur, wum, wo, _tile_of(R, 256))
    tss = _tile_of(R, 1024)
    ys = _moe_layer(hs, moe_w, tss, 1)
    y_sample = _final_norm(hs, ys, gf, tms)

    return (y_prompt.reshape(B, S, D), y_sample.reshape(DB, T, D),
            fk.reshape(1, B, S, FOX_HEADS, FOX_DIM), fv.reshape(1, B, S, FOX_HEADS, FOX_DIM),
            lf.reshape(1, B, S, FOX_HEADS), rs_p.reshape(1, B, RET_HEADS, RET_DIM, RET_DIM),
            mk.reshape(1, B, -1, MEM_HEADS, MEM_DIM), mv.reshape(1, B, -1, MEM_HEADS, MEM_DIM),
            fk_s.reshape(1, DB, T, FOX_HEADS, FOX_DIM), fv_s.reshape(1, DB, T, FOX_HEADS, FOX_DIM),
            lf_s.reshape(1, DB, T, FOX_HEADS), rs_s.reshape(1, DB, RET_HEADS, RET_DIM, RET_DIM))
```
